```python
import jax, jax.numpy as jnp
from jax import lax
import numpy as np

D_MODEL = 1024
BATCH = 4
SEQ = 4096
DEPTH = 1
DEC_BATCH = 8
DEC_SEQ = 64
PAST_LEN = 1024

CHUNK = 64
N_META = 16
D_RNN = 512
D_CONV = 512
N_RNN_HEADS = 8
RNN_HEAD_DIM = D_RNN // N_RNN_HEADS
RNN_CONV_W = 4
DW_CONV_W = 31
D_FF = 4 * D_MODEL
RGLRU_C = 8.0
EPS = 1e-6
D_IN = 2 * D_RNN + 2 * D_CONV

kernel_name = "hymba_griffin_conformer_stream_step"


def _rmsnorm(x, g):
    xf = x.astype(jnp.float32)
    y = xf * lax.rsqrt(jnp.mean(xf * xf, axis=-1, keepdims=True) + EPS)
    return (y * g.astype(jnp.float32)).astype(x.dtype)


def _layernorm(x, g, b):
    xf = x.astype(jnp.float32)
    mu = jnp.mean(xf, axis=-1, keepdims=True)
    xc = xf - mu
    y = xc * lax.rsqrt(jnp.mean(xc * xc, axis=-1, keepdims=True) + EPS)
    return (y * g.astype(jnp.float32) + b.astype(jnp.float32)).astype(x.dtype)


def _causal_dwconv(x, state, w, b):
    xpad = jnp.concatenate([state.astype(x.dtype), x], axis=1)
    out = lax.conv_general_dilated(
        xpad, w[:, None, :].astype(x.dtype), window_strides=(1,), padding="VALID",
        dimension_numbers=("NWC", "WIO", "NWC"), feature_group_count=x.shape[-1])
    return out + b.astype(x.dtype), xpad[:, -(w.shape[0] - 1):]


def _rglru(x, h0, w_r, b_r, w_i, b_i, lam):
    B, T, C = x.shape
    xf = x.astype(jnp.float32)
    xh = xf.reshape(B, T, N_RNN_HEADS, RNN_HEAD_DIM)
    r = jax.nn.sigmoid(jnp.einsum("bthd,hde->bthe", xh, w_r.astype(jnp.float32)).reshape(B, T, C)
                       + b_r.astype(jnp.float32))
    i = jax.nn.sigmoid(jnp.einsum("bthd,hde->bthe", xh, w_i.astype(jnp.float32)).reshape(B, T, C)
                       + b_i.astype(jnp.float32))
    log_a = -RGLRU_C * r * jax.nn.softplus(-lam.astype(jnp.float32))
    a = jnp.exp(log_a)
    mult = jnp.sqrt(jnp.maximum(-jnp.expm1(2.0 * log_a), 0.0))
    b_in = mult * i * xf

    def step(h, ab):
        a_t, b_t = ab
        h = a_t * h + b_t
        return h, h

    hT, hs = lax.scan(step, h0.astype(jnp.float32),
                      (jnp.swapaxes(a, 0, 1), jnp.swapaxes(b_in, 0, 1)))
    return jnp.swapaxes(hs, 0, 1).astype(x.dtype), hT.astype(h0.dtype)


def _layer(x, conv_state, h_state, dw_state, norm_mix, w_in, rnn_conv_w, rnn_conv_b,
           w_gate_r, b_gate_r, w_gate_i, b_gate_i, rglru_lambda, dw_w, dw_b, ln_conv_g,
           ln_conv_b, out_norm_rnn, out_norm_conv, w_out, norm_mlp, w_up, w_down):
    hn = _rmsnorm(x, norm_mix)
    z = hn @ w_in
    xr, gate, glu_v, glu_g = jnp.split(z, [D_RNN, 2 * D_RNN, 2 * D_RNN + D_CONV], axis=-1)
    xr_c, new_conv = _causal_dwconv(xr, conv_state, rnn_conv_w, rnn_conv_b)
    hr, new_h = _rglru(xr_c, h_state, w_gate_r, b_gate_r, w_gate_i, b_gate_i, rglru_lambda)
    y_rnn = hr * jax.nn.gelu(gate)
    v = glu_v * jax.nn.sigmoid(glu_g)
    vc, new_dw = _causal_dwconv(v, dw_state, dw_w, dw_b)
    y_conv = jax.nn.silu(_layernorm(vc, ln_conv_g, ln_conv_b))
    mix = jnp.concatenate([_rmsnorm(y_rnn, out_norm_rnn), _rmsnorm(y_conv, out_norm_conv)], axis=-1)
    x = x + mix @ w_out
    hm = _rmsnorm(x, norm_mlp)
    x = x + jnp.square(jax.nn.relu(hm @ w_up)) @ w_down
    return x, new_conv, new_h, new_dw


def setup_inputs(seed: int = 0) -> dict:
    key = jax.random.key(seed)
    ks = jax.random.split(key, 32)
    nrm = jax.random.normal
    f32 = jnp.float32
    u = jax.random.uniform(ks[10], (DEPTH, D_RNN), f32, 0.9, 0.999)
    a_base = u ** (1.0 / RGLRU_C)
    lam = jnp.log(a_base) - jnp.log1p(-a_base)
    return {
        "x_prompt": nrm(ks[0], (BATCH, SEQ, D_MODEL), f32),
        "x_sample": nrm(ks[1], (DEC_BATCH, DEC_SEQ, D_MODEL), f32),
        "state_rglru_conv": nrm(ks[2], (DEPTH, DEC_BATCH, RNN_CONV_W - 1, D_RNN), f32),
        "state_rglru_h": 0.5 * nrm(ks[3], (DEPTH, DEC_BATCH, D_RNN), f32),
        "state_dwconv": 0.5 * nrm(ks[4], (DEPTH, DEC_BATCH, DW_CONV_W - 1, D_CONV), f32),
        "meta_tokens": nrm(ks[5], (N_META, D_MODEL), f32),
        "norm_mix": 1.0 + 0.05 * nrm(ks[6], (DEPTH, D_MODEL), f32),
        "w_in": nrm(ks[7], (DEPTH, D_MODEL, D_IN), f32) * D_MODEL ** -0.5,
        "rnn_conv_w": nrm(ks[8], (DEPTH, RNN_CONV_W, D_RNN), f32) * RNN_CONV_W ** -0.5,
        "rnn_conv_b": 0.01 * nrm(ks[9], (DEPTH, D_RNN), f32),
        "w_gate_r": nrm(ks[11], (DEPTH, N_RNN_HEADS, RNN_HEAD_DIM, RNN_HEAD_DIM), f32) * RNN_HEAD_DIM ** -0.5,
        "b_gate_r": 0.01 * nrm(ks[12], (DEPTH, D_RNN), f32),
        "w_gate_i": nrm(ks[13], (DEPTH, N_RNN_HEADS, RNN_HEAD_DIM, RNN_HEAD_DIM), f32) * RNN_HEAD_DIM ** -0.5,
        "b_gate_i": 0.01 * nrm(ks[14], (DEPTH, D_RNN), f32),
        "rglru_lambda": lam,
        "dw_w": nrm(ks[15], (DEPTH, DW_CONV_W, D_CONV), f32) * DW_CONV_W ** -0.5,
        "dw_b": 0.01 * nrm(ks[16], (DEPTH, D_CONV), f32),
        "ln_conv_g": 1.0 + 0.05 * nrm(ks[17], (DEPTH, D_CONV), f32),
        "ln_conv_b": 0.01 * nrm(ks[18], (DEPTH, D_CONV), f32),
        "out_norm_rnn": 1.0 + 0.05 * nrm(ks[19], (DEPTH, D_RNN), f32),
        "out_norm_conv": 1.0 + 0.05 * nrm(ks[20], (DEPTH, D_CONV), f32),
        "w_out": nrm(ks[21], (DEPTH, D_RNN + D_CONV, D_MODEL), f32) * (D_RNN + D_CONV) ** -0.5,
        "norm_mlp": 1.0 + 0.05 * nrm(ks[22], (DEPTH, D_MODEL), f32),
        "w_up": nrm(ks[23], (DEPTH, D_MODEL, D_FF), f32) * D_MODEL ** -0.5,
        "w_down": nrm(ks[24], (DEPTH, D_FF, D_MODEL), f32) * D_FF ** -0.5,
        "norm_final": 1.0 + 0.05 * nrm(ks[25], (D_MODEL,), f32),
    }


def reference(x_prompt, x_sample, state_rglru_conv, state_rglru_h, state_dwconv, meta_tokens,
              norm_mix, w_in, rnn_conv_w, rnn_conv_b, w_gate_r, b_gate_r, w_gate_i, b_gate_i,
              rglru_lambda, dw_w, dw_b, ln_conv_g, ln_conv_b, out_norm_rnn, out_norm_conv,
              w_out, norm_mlp, w_up, w_down, norm_final):
    bp = x_prompt.shape[0]
    dt = x_prompt.dtype
    meta = jnp.broadcast_to(meta_tokens.astype(dt)[None], (bp, N_META, D_MODEL))
    xp = jnp.concatenate([meta, x_prompt], axis=1)
    xs = x_sample
    conv_p, h_p, dw_p, conv_s, h_s, dw_s = [], [], [], [], [], []
    for l in range(DEPTH):
        params = (norm_mix[l], w_in[l], rnn_conv_w[l], rnn_conv_b[l], w_gate_r[l], b_gate_r[l],
                  w_gate_i[l], b_gate_i[l], rglru_lambda[l], dw_w[l], dw_b[l], ln_conv_g[l],
                  ln_conv_b[l], out_norm_rnn[l], out_norm_conv[l], w_out[l], norm_mlp[l],
                  w_up[l], w_down[l])
        xp, c_p, hh_p, d_p = _layer(
            xp, jnp.zeros((bp, RNN_CONV_W - 1, D_RNN), dt), jnp.zeros((bp, D_RNN), dt),
            jnp.zeros((bp, DW_CONV_W - 1, D_CONV), dt), *params)
        xs, c_s, hh_s, d_s = _layer(xs, state_rglru_conv[l], state_rglru_h[l], state_dwconv[l], *params)
        conv_p.append(c_p); h_p.append(hh_p); dw_p.append(d_p)
        conv_s.append(c_s); h_s.append(hh_s); dw_s.append(d_s)
    y_prompt = _rmsnorm(xp, norm_final)[:, N_META:]
    y_sample = _rmsnorm(xs, norm_final)
    new_rglru_conv_prompt = jnp.stack(conv_p)
    new_rglru_h_prompt = jnp.stack(h_p)
    new_dwconv_prompt = jnp.stack(dw_p)
    new_rglru_conv_sample = jnp.stack(conv_s)
    new_rglru_h_sample = jnp.stack(h_s)
    new_dwconv_sample = jnp.stack(dw_s)
    return (y_prompt, y_sample, new_rglru_conv_prompt, new_rglru_h_prompt, new_dwconv_prompt,
            new_rglru_conv_sample, new_rglru_h_sample, new_dwconv_sample)
```

```python
import functools
import math

import jax
import jax.numpy as jnp
from jax import lax
from jax.experimental import pallas as pl
from jax.experimental.pallas import tpu as pltpu

EPS = 1e-6
RGLRU_C = 8.0
SUBLANES = 8
MXU_DIM = 256
VMEM_LIMIT_BYTES = 60000 * 1024


def _rms(x, g):
    ms = jnp.mean(x * x, axis=-1, keepdims=True)
    return x * lax.rsqrt(ms + EPS) * g


def _sigmoid(x):
    return 1.0 / (1.0 + jnp.exp(-x))


def _gelu_tanh(x):
    cdf = 0.5 * (1.0 + jnp.tanh(math.sqrt(2.0 / math.pi) * (x + 0.044715 * (x * x * x))))
    return x * cdf


def _bf16_dot(a, b):
    return jnp.dot(a.astype(jnp.bfloat16), b, preferred_element_type=jnp.float32)


def _layer_kernel(x_ref, conv0_ref, h0_ref, dw0_ref,
                  norm_mix_ref, w_in_ref, cw_ref, cb_ref, w_gate_ref, b_r_ref, b_i_ref, lam_ref,
                  dww_ref, dwb_ref, ln_g_ref, ln_b_ref, onr_ref, onc_ref,
                  w_out_ref, norm_mlp_ref, w_up_ref, w_down_ref, norm_final_ref,
                  *rest, n_streams, tm, emit_y, final_norm, rnn_conv_w, dw_conv_w, ff_chunk, dw_rows):
    if emit_y:
        y_ref, conv_out_ref, h_out_ref, dw_out_ref = rest[:4]
        scratch = rest[4:]
    else:
        conv_out_ref, h_out_ref, dw_out_ref = rest[:3]
        scratch = rest[3:]
    cbuf, vbuf, hcar, a_buf, b_buf, h_buf, vc_buf = scratch

    S, TM = n_streams, tm
    M = S * TM
    d_rnn = cw_ref.shape[-1]
    d_conv = dww_ref.shape[-1]
    cpad = SUBLANES
    vpad = vbuf.shape[1] - TM
    c_hist = rnn_conv_w - 1
    v_hist = dw_conv_w - 1
    t = pl.program_id(1)

    @pl.when(t == 0)
    def _init():
        cbuf[:, 0:cpad - c_hist, :] = jnp.zeros((S, cpad - c_hist, d_rnn), jnp.float32)
        cbuf[:, cpad - c_hist:cpad, :] = conv0_ref[...]
        vbuf[:, 0:vpad - v_hist, :] = jnp.zeros((S, vpad - v_hist, d_conv), jnp.float32)
        vbuf[:, vpad - v_hist:vpad, :] = dw0_ref[...]
        hcar[...] = jnp.broadcast_to(h0_ref[...], hcar.shape)

    x = x_ref[...].reshape(M, x_ref.shape[-1])
    hn = _rms(x, norm_mix_ref[...])
    z = _bf16_dot(hn, w_in_ref[...])
    xr = z[:, 0:d_rnn]
    gate = z[:, d_rnn:2 * d_rnn]
    glu_v = z[:, 2 * d_rnn:2 * d_rnn + d_conv]
    glu_g = z[:, 2 * d_rnn + d_conv:]

    xrc_parts = []
    for s in range(S):
        cbuf[s, cpad:cpad + TM, :] = xr[s * TM:(s + 1) * TM, :]
        acc = cbuf[s, cpad - c_hist:cpad - c_hist + TM, :] * cw_ref[0:1, :]
        for k in range(1, rnn_conv_w):
            acc = acc + cbuf[s, cpad - c_hist + k:cpad - c_hist + k + TM, :] * cw_ref[k:k + 1, :]
        xrc_parts.append(acc + cb_ref[...])
        conv_out_ref[s] = cbuf[s, cpad + TM - c_hist:cpad + TM, :]
        cbuf[s, 0:cpad, :] = cbuf[s, TM:TM + cpad, :]
    xr_c = xrc_parts[0] if S == 1 else jnp.concatenate(xrc_parts, axis=0)

    xb = xr_c.astype(jnp.bfloat16)
    r_parts, i_parts = [], []
    for j in range(d_rnn // MXU_DIM):
        ri = jnp.dot(xb[:, j * MXU_DIM:(j + 1) * MXU_DIM], w_gate_ref[j], preferred_element_type=jnp.float32)
        r_parts.append(ri[:, :MXU_DIM])
        i_parts.append(ri[:, MXU_DIM:])
    r = _sigmoid(jnp.concatenate(r_parts, axis=-1) + b_r_ref[...])
    ig = _sigmoid(jnp.concatenate(i_parts, axis=-1) + b_i_ref[...])
    nl = -lam_ref[...]
    softplus_nl = jnp.maximum(nl, 0.0) + jnp.log1p(jnp.exp(-jnp.abs(nl)))
    log_a = (-RGLRU_C * r) * softplus_nl
    a = jnp.exp(log_a)
    mult = jnp.sqrt(jnp.maximum(1.0 - a * a, 0.0))
    a_buf[...] = a
    b_buf[...] = mult * ig * xr_c

    row = lax.broadcasted_iota(jnp.int32, (SUBLANES, d_rnn), 0)

    def scan_group(j, carry):
        out = []
        for s in range(S):
            base = pl.multiple_of(s * TM + j * SUBLANES, SUBLANES)
            pa = a_buf[pl.ds(base, SUBLANES), :]
            pb = b_buf[pl.ds(base, SUBLANES), :]
            for sh in (1, 2, 4):
                keep = row >= sh
                pa_s = jnp.where(keep, pltpu.roll(pa, sh, 0), 1.0)
                pb_s = jnp.where(keep, pltpu.roll(pb, sh, 0), 0.0)
                pb = pa * pb_s + pb
                pa = pa * pa_s
            h = pa * carry[s] + pb
            h_buf[pl.ds(base, SUBLANES), :] = h
            out.append(jnp.broadcast_to(h[SUBLANES - 1:SUBLANES, :], (SUBLANES, d_rnn)))
        return tuple(out)

    carry = lax.fori_loop(0, TM // SUBLANES, scan_group, tuple(hcar[s] for s in range(S)),
                          unroll=min(4, TM // SUBLANES))
    for s in range(S):
        hcar[s] = carry[s]
        h_out_ref[s] = carry[s][0:1, :]

    v = glu_v * _sigmoid(glu_g)
    for s in range(S):
        vbuf[s, vpad:vpad + TM, :] = v[s * TM:(s + 1) * TM, :]

    for i in range(TM // dw_rows):
        base = i * dw_rows + vpad - v_hist
        for s in range(S):
            acc = vbuf[s, base:base + dw_rows, :] * dww_ref[0:1, :]
            for k in range(1, dw_conv_w):
                acc = acc + vbuf[s, base + k:base + k + dw_rows, :] * dww_ref[k:k + 1, :]
            vc_buf[s * TM + i * dw_rows:s * TM + (i + 1) * dw_rows, :] = acc + dwb_ref[...]
    for s in range(S):
        dw_out_ref[s] = vbuf[s, vpad + TM - v_hist:vpad + TM, :]
        vbuf[s, 0:vpad, :] = vbuf[s, TM:TM + vpad, :]

    if not emit_y:
        return

    y_rnn = h_buf[...] * _gelu_tanh(gate)
    vc = vc_buf[...]
    mu = jnp.mean(vc, axis=-1, keepdims=True)
    vcc = vc - mu
    ln = vcc * lax.rsqrt(jnp.mean(vcc * vcc, axis=-1, keepdims=True) + EPS) * ln_g_ref[...] + ln_b_ref[...]
    y_conv = ln * _sigmoid(ln)
    mix = jnp.concatenate([_rms(y_rnn, onr_ref[...]), _rms(y_conv, onc_ref[...])], axis=-1)
    x1 = x + _bf16_dot(mix, w_out_ref[...])

    hm = _rms(x1, norm_mlp_ref[...]).astype(jnp.bfloat16)
    d_ff = w_up_ref.shape[-1]
    x2 = x1
    for c in range(d_ff // ff_chunk):
        hc = jnp.dot(hm, w_up_ref[:, c * ff_chunk:(c + 1) * ff_chunk], preferred_element_type=jnp.float32)
        hc = jnp.square(jnp.maximum(hc, 0.0))
        x2 = x2 + _bf16_dot(hc, w_down_ref[c * ff_chunk:(c + 1) * ff_chunk, :])
    if final_norm:
        x2 = _rms(x2, norm_final_ref[...])
    y_ref[...] = x2.reshape(y_ref.shape)


def _const_spec(arr):
    nd = arr.ndim
    return pl.BlockSpec(arr.shape, lambda b, t, _nd=nd: (0,) * _nd, pipeline_mode=pl.Buffered(1))


def _run_layer(x, conv0, h0, dw0, params, *, n_streams, tm, emit_y, final_norm):
    B, T, D = x.shape
    S, TM = n_streams, tm
    assert B % S == 0 and T % TM == 0 and TM % SUBLANES == 0
    rnn_conv_w, d_rnn = params["cw"].shape
    dw_conv_w, d_conv = params["dww"].shape
    assert d_rnn % MXU_DIM == 0
    vpad = -(-(dw_conv_w - 1) // SUBLANES) * SUBLANES
    dw_rows = min(32, TM)
    assert TM % dw_rows == 0
    d_ff = params["w_up"].shape[-1]
    ff_chunk = min(1024, d_ff)

    def state_spec(arr):
        blk = (S,) + arr.shape[1:]
        if arr.shape[0] == B:
            return pl.BlockSpec(blk, lambda b, t: (b, 0, 0))
        assert arr.shape[0] == 1 and S == 1
        return pl.BlockSpec(blk, lambda b, t: (0, 0, 0))

    order = ["norm_mix", "w_in", "cw", "cb", "w_gate", "b_r", "b_i", "lam", "dww", "dwb", "ln_g", "ln_b",
             "onr", "onc", "w_out", "norm_mlp", "w_up", "w_down", "norm_final"]
    weights = [params[k] for k in order]
    in_specs = [pl.BlockSpec((S, TM, D), lambda b, t: (b, t, 0)),
                state_spec(conv0), state_spec(h0), state_spec(dw0)] + [_const_spec(w) for w in weights]

    f32 = jnp.float32
    out_shape = [jax.ShapeDtypeStruct((B, rnn_conv_w - 1, d_rnn), f32),
                 jax.ShapeDtypeStruct((B, 1, d_rnn), f32),
                 jax.ShapeDtypeStruct((B, dw_conv_w - 1, d_conv), f32)]
    out_specs = [pl.BlockSpec((S, rnn_conv_w - 1, d_rnn), lambda b, t: (b, 0, 0)),
                 pl.BlockSpec((S, 1, d_rnn), lambda b, t: (b, 0, 0)),
                 pl.BlockSpec((S, dw_conv_w - 1, d_conv), lambda b, t: (b, 0, 0))]
    if emit_y:
        out_shape = [jax.ShapeDtypeStruct((B, T, D), x.dtype)] + out_shape
        out_specs = [pl.BlockSpec((S, TM, D), lambda b, t: (b, t, 0))] + out_specs

    M = S * TM
    scratch_shapes = [
        pltpu.VMEM((S, SUBLANES + TM, d_rnn), f32),
        pltpu.VMEM((S, vpad + TM, d_conv), f32),
        pltpu.VMEM((S, SUBLANES, d_rnn), f32),
        pltpu.VMEM((M, d_rnn), f32),
        pltpu.VMEM((M, d_rnn), f32),
        pltpu.VMEM((M, d_rnn), f32),
        pltpu.VMEM((M, d_conv), f32),
    ]
    kern = functools.partial(_layer_kernel, n_streams=S, tm=TM, emit_y=emit_y, final_norm=final_norm,
                             rnn_conv_w=rnn_conv_w, dw_conv_w=dw_conv_w, ff_chunk=ff_chunk, dw_rows=dw_rows)
    return pl.pallas_call(
        kern,
        grid=(B // S, T // TM),
        in_specs=in_specs,
        out_specs=out_specs,
        out_shape=out_shape,
        scratch_shapes=scratch_shapes,
        compiler_params=pltpu.CompilerParams(
            dimension_semantics=("arbitrary", "arbitrary"),
            vmem_limit_bytes=VMEM_LIMIT_BYTES),
        name=f"hybrid_layer_s{S}_t{TM}",
    )(x, conv0, h0, dw0, *weights)


def _block_diag_gates(w_r, w_i):
    n_heads, hd, _ = w_r.shape
    per = MXU_DIM // hd
    n_slab = n_heads // per
    on_diag = jnp.eye(per, dtype=bool)[None, :, None, :, None]

    def slabs(w):
        w = w.reshape(n_slab, per, hd, hd)
        bd = jnp.where(on_diag, w[:, :, :, None, :], 0.0)
        return bd.reshape(n_slab, MXU_DIM, MXU_DIM)

    return jnp.concatenate([slabs(w_r), slabs(w_i)], axis=-1).astype(jnp.bfloat16)


def kernel(x_prompt, x_sample, state_rglru_conv, state_rglru_h, state_dwconv, meta_tokens, norm_mix, w_in,
           rnn_conv_w, rnn_conv_b, w_gate_r, b_gate_r, w_gate_i, b_gate_i, rglru_lambda, dw_w, dw_b, ln_conv_g,
           ln_conv_b, out_norm_rnn, out_norm_conv, w_out, norm_mlp, w_up, w_down, norm_final):
    depth = norm_mix.shape[0]
    bp = x_prompt.shape[0]
    bf16 = jnp.bfloat16
    d_rnn = rnn_conv_w.shape[-1]
    d_conv = dw_w.shape[-1]
    c_hist = rnn_conv_w.shape[1] - 1
    v_hist = dw_w.shape[1] - 1

    xm = meta_tokens[None].astype(x_prompt.dtype)
    xp, xs = x_prompt, x_sample
    outs_p, outs_s = [], []
    for l in range(depth):
        last = l == depth - 1
        params = dict(
            norm_mix=norm_mix[l][None], w_in=w_in[l].astype(bf16), cw=rnn_conv_w[l], cb=rnn_conv_b[l][None],
            w_gate=_block_diag_gates(w_gate_r[l], w_gate_i[l]), b_r=b_gate_r[l][None], b_i=b_gate_i[l][None],
            lam=rglru_lambda[l][None], dww=dw_w[l], dwb=dw_b[l][None], ln_g=ln_conv_g[l][None],
            ln_b=ln_conv_b[l][None], onr=out_norm_rnn[l][None], onc=out_norm_conv[l][None],
            w_out=w_out[l].astype(bf16), norm_mlp=norm_mlp[l][None], w_up=w_up[l].astype(bf16),
            w_down=w_down[l].astype(bf16), norm_final=norm_final[None])
        f32 = jnp.float32
        zero_state = (jnp.zeros((1, c_hist, d_rnn), f32), jnp.zeros((1, 1, d_rnn), f32),
                      jnp.zeros((1, v_hist, d_conv), f32))
        res_m = _run_layer(xm, *zero_state, params, n_streams=1, tm=xm.shape[1], emit_y=not last,
                           final_norm=False)
        if not last:
            xm, res_m = res_m[0], res_m[1:]
        res_p = _run_layer(xp, *res_m, params, n_streams=1, tm=min(512, xp.shape[1]), emit_y=True,
                           final_norm=last)
        res_s = _run_layer(xs, state_rglru_conv[l], state_rglru_h[l][:, None], state_dwconv[l], params,
                           n_streams=xs.shape[0], tm=xs.shape[1], emit_y=True, final_norm=last)
        xp, xs = res_p[0], res_s[0]
        outs_p.append(res_p[1:])
        outs_s.append(res_s[1:])

    def stack(outs, i):
        return jnp.stack([o[i] for o in outs])

    return (xp, xs,
            stack(outs_p, 0), stack(outs_p, 1)[:, :, 0], stack(outs_p, 2),
            stack(outs_s, 0), stack(outs_s, 1)[:, :, 0], stack(outs_s, 2))
```

```python
import functools
import math

import jax
import jax.numpy as jnp
from jax import lax
from jax.experimental import pallas as pl
from jax.experimental.pallas import tpu as pltpu

EPS = 1e-6
RGLRU_C = 8.0
SUBLANES = 8
LANES = 128
MXU_DIM = 256
VMEM_LIMIT_BYTES = 60000 * 1024


def _rms(x, g):
    ms = jnp.mean(x * x, axis=-1, keepdims=True)
    return x * lax.rsqrt(ms + EPS) * g


def _sigmoid(x):
    return 1.0 / (1.0 + jnp.exp(-x))


def _gelu_tanh(x):
    cdf = 0.5 * (1.0 + jnp.tanh(math.sqrt(2.0 / math.pi) * (x + 0.044715 * (x * x * x))))
    return x * cdf


def _bf16_dot(a, b):
    return jnp.dot(a.astype(jnp.bfloat16), b, preferred_element_type=jnp.float32)


def _to_slabs(dst, idx, x):
    for c in range(x.shape[-1] // LANES):
        dst[idx + (c,)] = x[:, c * LANES:(c + 1) * LANES]


def _from_slabs(buf):
    return jnp.concatenate([buf[c] for c in range(buf.shape[0])], axis=-1)


def _strided_causal_conv(win, s, w_ref, b_ref, out, *, taps, lead, tm, rows):
    half = rows // 2
    for c in range(win.shape[1]):
        lanes = slice(c * LANES, (c + 1) * LANES)
        for t0 in [i * rows + par for i in range(tm // rows) for par in range(2)]:
            acc = win[s, c, pl.ds(lead + t0, half, stride=2), :] * w_ref[0:1, lanes]
            for k in range(1, taps):
                acc = acc + win[s, c, pl.ds(lead + t0 + k, half, stride=2), :] * w_ref[k:k + 1, lanes]
            out[c, pl.ds(s * tm + t0, half, stride=2), :] = acc + b_ref[:, lanes]


def _layer_kernel(x_ref, conv0_ref, h0_ref, dw0_ref,
                  norm_mix_ref, w_in_ref, cw_ref, cb_ref, w_gate_ref, b_r_ref, b_i_ref, lam_ref,
                  dww_ref, dwb_ref, ln_g_ref, ln_b_ref, onr_ref, onc_ref,
                  w_out_ref, norm_mlp_ref, w_up_ref, w_down_ref, norm_final_ref,
                  *rest, n_streams, tm, emit_y, final_norm, rnn_conv_w, dw_conv_w, ff_chunk, conv_rows):
    if emit_y:
        y_ref, conv_out_ref, h_out_ref, dw_out_ref = rest[:4]
        scratch = rest[4:]
    else:
        conv_out_ref, h_out_ref, dw_out_ref = rest[:3]
        scratch = rest[3:]
    cbuf, vbuf, hcar, a_buf, b_buf, h_buf, xc_buf, vc_buf = scratch

    S, TM = n_streams, tm
    M = S * TM
    d_rnn = cw_ref.shape[-1]
    d_conv = dww_ref.shape[-1]
    cpad = cbuf.shape[2] - TM
    vpad = vbuf.shape[2] - TM
    c_hist = rnn_conv_w - 1
    v_hist = dw_conv_w - 1
    t = pl.program_id(1)

    @pl.when(t == 0)
    def _init():
        cbuf[:, :, 0:cpad - c_hist, :] = jnp.zeros((S, d_rnn // LANES, cpad - c_hist, LANES), jnp.float32)
        vbuf[:, :, 0:vpad - v_hist, :] = jnp.zeros((S, d_conv // LANES, vpad - v_hist, LANES), jnp.float32)
        for c in range(d_rnn // LANES):
            cbuf[:, c, cpad - c_hist:cpad, :] = conv0_ref[:, :, c * LANES:(c + 1) * LANES]
        for c in range(d_conv // LANES):
            vbuf[:, c, vpad - v_hist:vpad, :] = dw0_ref[:, :, c * LANES:(c + 1) * LANES]
        hcar[...] = jnp.broadcast_to(h0_ref[...], hcar.shape)

    x = x_ref[...].reshape(M, x_ref.shape[-1])
    hn = _rms(x, norm_mix_ref[...])
    z = _bf16_dot(hn, w_in_ref[...])
    xr = z[:, 0:d_rnn]
    gate = z[:, d_rnn:2 * d_rnn]
    glu_v = z[:, 2 * d_rnn:2 * d_rnn + d_conv]
    glu_g = z[:, 2 * d_rnn + d_conv:]

    for s in range(S):
        _to_slabs(cbuf.at[:, :, cpad:cpad + TM, :], (s,), xr[s * TM:(s + 1) * TM, :])
        _strided_causal_conv(cbuf, s, cw_ref, cb_ref, xc_buf, taps=rnn_conv_w, lead=cpad - c_hist, tm=TM,
                             rows=conv_rows)
        for c in range(d_rnn // LANES):
            conv_out_ref[s, :, c * LANES:(c + 1) * LANES] = cbuf[s, c, cpad + TM - c_hist:cpad + TM, :]
            cbuf[s, c, 0:cpad, :] = cbuf[s, c, TM:TM + cpad, :]
    xr_c = _from_slabs(xc_buf)

    xb = xr_c.astype(jnp.bfloat16)
    r_parts, i_parts = [], []
    for j in range(d_rnn // MXU_DIM):
        ri = jnp.dot(xb[:, j * MXU_DIM:(j + 1) * MXU_DIM], w_gate_ref[j], preferred_element_type=jnp.float32)
        r_parts.append(ri[:, :MXU_DIM])
        i_parts.append(ri[:, MXU_DIM:])
    r = _sigmoid(jnp.concatenate(r_parts, axis=-1) + b_r_ref[...])
    ig = _sigmoid(jnp.concatenate(i_parts, axis=-1) + b_i_ref[...])
    nl = -lam_ref[...]
    softplus_nl = jnp.maximum(nl, 0.0) + jnp.log1p(jnp.exp(-jnp.abs(nl)))
    log_a = (-RGLRU_C * r) * softplus_nl
    a = jnp.exp(log_a)
    mult = jnp.sqrt(jnp.maximum(1.0 - a * a, 0.0))
    a_buf[...] = a
    b_buf[...] = mult * ig * xr_c

    row = lax.broadcasted_iota(jnp.int32, (SUBLANES, d_rnn), 0)

    def scan_group(j, carry):
        out = []
        for s in range(S):
            base = pl.multiple_of(s * TM + j * SUBLANES, SUBLANES)
            pa = a_buf[pl.ds(base, SUBLANES), :]
            pb = b_buf[pl.ds(base, SUBLANES), :]
            for sh in (1, 2, 4):
                keep = row >= sh
                pa_s = jnp.where(keep, pltpu.roll(pa, sh, 0), 1.0)
                pb_s = jnp.where(keep, pltpu.roll(pb, sh, 0), 0.0)
                pb = pa * pb_s + pb
                pa = pa * pa_s
            h = pa * carry[s] + pb
            h_buf[pl.ds(base, SUBLANES), :] = h
            out.append(jnp.broadcast_to(h[SUBLANES - 1:SUBLANES, :], (SUBLANES, d_rnn)))
        return tuple(out)

    carry = lax.fori_loop(0, TM // SUBLANES, scan_group, tuple(hcar[s] for s in range(S)),
                          unroll=min(4, TM // SUBLANES))
    for s in range(S):
        hcar[s] = carry[s]
        h_out_ref[s] = carry[s][0:1, :]

    v = glu_v * _sigmoid(glu_g)
    for s in range(S):
        _to_slabs(vbuf.at[:, :, vpad:vpad + TM, :], (s,), v[s * TM:(s + 1) * TM, :])
        _strided_causal_conv(vbuf, s, dww_ref, dwb_ref, vc_buf, taps=dw_conv_w, lead=vpad - v_hist, tm=TM,
                             rows=conv_rows)
        for c in range(d_conv // LANES):
            dw_out_ref[s, :, c * LANES:(c + 1) * LANES] = vbuf[s, c, vpad + TM - v_hist:vpad + TM, :]
            vbuf[s, c, 0:vpad, :] = vbuf[s, c, TM:TM + vpad, :]

    if not emit_y:
        return

    y_rnn = h_buf[...] * _gelu_tanh(gate)
    vc = _from_slabs(vc_buf)
    mu = jnp.mean(vc, axis=-1, keepdims=True)
    vcc = vc - mu
    ln = vcc * lax.rsqrt(jnp.mean(vcc * vcc, axis=-1, keepdims=True) + EPS) * ln_g_ref[...] + ln_b_ref[...]
    y_conv = ln * _sigmoid(ln)
    mix = jnp.concatenate([_rms(y_rnn, onr_ref[...]), _rms(y_conv, onc_ref[...])], axis=-1)
    x1 = x + _bf16_dot(mix, w_out_ref[...])

    hm = _rms(x1, norm_mlp_ref[...]).astype(jnp.bfloat16)
    d_ff = w_up_ref.shape[-1]
    x2 = x1
    for c in range(d_ff // ff_chunk):
        hc = jnp.dot(hm, w_up_ref[:, c * ff_chunk:(c + 1) * ff_chunk], preferred_element_type=jnp.float32)
        hc = jnp.square(jnp.maximum(hc, 0.0))
        x2 = x2 + _bf16_dot(hc, w_down_ref[c * ff_chunk:(c + 1) * ff_chunk, :])
    if final_norm:
        x2 = _rms(x2, norm_final_ref[...])
    y_ref[...] = x2.reshape(y_ref.shape)


def _const_spec(arr):
    nd = arr.ndim
    return pl.BlockSpec(arr.shape, lambda b, t, _nd=nd: (0,) * _nd, pipeline_mode=pl.Buffered(1))


def _run_layer(x, conv0, h0, dw0, params, *, n_streams, tm, emit_y, final_norm):
    B, T, D = x.shape
    S, TM = n_streams, tm
    assert B % S == 0 and T % TM == 0 and TM % SUBLANES == 0
    rnn_conv_w, d_rnn = params["cw"].shape
    dw_conv_w, d_conv = params["dww"].shape
    assert d_rnn % MXU_DIM == 0 and d_conv % LANES == 0
    cpad = -(-(rnn_conv_w - 1) // SUBLANES) * SUBLANES
    vpad = -(-(dw_conv_w - 1) // SUBLANES) * SUBLANES
    conv_rows = min(64, TM)
    assert TM % conv_rows == 0 and conv_rows % (2 * SUBLANES) == 0
    d_ff = params["w_up"].shape[-1]
    ff_chunk = min(1024, d_ff)

    def state_spec(arr):
        blk = (S,) + arr.shape[1:]
        if arr.shape[0] == B:
            return pl.BlockSpec(blk, lambda b, t: (b, 0, 0))
        assert arr.shape[0] == 1 and S == 1
        return pl.BlockSpec(blk, lambda b, t: (0, 0, 0))

    order = ["norm_mix", "w_in", "cw", "cb", "w_gate", "b_r", "b_i", "lam", "dww", "dwb", "ln_g", "ln_b",
             "onr", "onc", "w_out", "norm_mlp", "w_up", "w_down", "norm_final"]
    weights = [params[k] for k in order]
    in_specs = [pl.BlockSpec((S, TM, D), lambda b, t: (b, t, 0)),
                state_spec(conv0), state_spec(h0), state_spec(dw0)] + [_const_spec(w) for w in weights]

    f32 = jnp.float32
    out_shape = [jax.ShapeDtypeStruct((B, rnn_conv_w - 1, d_rnn), f32),
                 jax.ShapeDtypeStruct((B, 1, d_rnn), f32),
                 jax.ShapeDtypeStruct((B, dw_conv_w - 1, d_conv), f32)]
    out_specs = [pl.BlockSpec((S, rnn_conv_w - 1, d_rnn), lambda b, t: (b, 0, 0)),
                 pl.BlockSpec((S, 1, d_rnn), lambda b, t: (b, 0, 0)),
                 pl.BlockSpec((S, dw_conv_w - 1, d_conv), lambda b, t: (b, 0, 0))]
    if emit_y:
        out_shape = [jax.ShapeDtypeStruct((B, T, D), x.dtype)] + out_shape
        out_specs = [pl.BlockSpec((S, TM, D), lambda b, t: (b, t, 0))] + out_specs

    M = S * TM
    scratch_shapes = [
        pltpu.VMEM((S, d_rnn // LANES, cpad + TM, LANES), f32),
        pltpu.VMEM((S, d_conv // LANES, vpad + TM, LANES), f32),
        pltpu.VMEM((S, SUBLANES, d_rnn), f32),
        pltpu.VMEM((M, d_rnn), f32),
        pltpu.VMEM((M, d_rnn), f32),
        pltpu.VMEM((M, d_rnn), f32),
        pltpu.VMEM((d_rnn // LANES, M, LANES), f32),
        pltpu.VMEM((d_conv // LANES, M, LANES), f32),
    ]
    kern = functools.partial(_layer_kernel, n_streams=S, tm=TM, emit_y=emit_y, final_norm=final_norm,
                             rnn_conv_w=rnn_conv_w, dw_conv_w=dw_conv_w, ff_chunk=ff_chunk, conv_rows=conv_rows)
    return pl.pallas_call(
        kern,
        grid=(B // S, T // TM),
        in_specs=in_specs,
        out_specs=out_specs,
        out_shape=out_shape,
        scratch_shapes=scratch_shapes,
        compiler_params=pltpu.CompilerParams(
            dimension_semantics=("arbitrary", "arbitrary"),
            vmem_limit_bytes=VMEM_LIMIT_BYTES),
        name=f"hybrid_layer_s{S}_t{TM}",
    )(x, conv0, h0, dw0, *weights)


def _block_diag_gates(w_r, w_i):
    n_heads, hd, _ = w_r.shape
    per = MXU_DIM // hd
    n_slab = n_heads // per
    on_diag = jnp.eye(per, dtype=bool)[None, :, None, :, None]

    def slabs(w):
        w = w.reshape(n_slab, per, hd, hd)
        bd = jnp.where(on_diag, w[:, :, :, None, :], 0.0)
        return bd.reshape(n_slab, MXU_DIM, MXU_DIM)

    return jnp.concatenate([slabs(w_r), slabs(w_i)], axis=-1).astype(jnp.bfloat16)


def kernel(x_prompt, x_sample, state_rglru_conv, state_rglru_h, state_dwconv, meta_tokens, norm_mix, w_in,
           rnn_conv_w, rnn_conv_b, w_gate_r, b_gate_r, w_gate_i, b_gate_i, rglru_lambda, dw_w, dw_b, ln_conv_g,
           ln_conv_b, out_norm_rnn, out_norm_conv, w_out, norm_mlp, w_up, w_down, norm_final):
    depth = norm_mix.shape[0]
    bp = x_prompt.shape[0]
    bf16 = jnp.bfloat16
    d_rnn = rnn_conv_w.shape[-1]
    d_conv = dw_w.shape[-1]
    c_hist = rnn_conv_w.shape[1] - 1
    v_hist = dw_w.shape[1] - 1

    xm = meta_tokens[None].astype(x_prompt.dtype)
    xp, xs = x_prompt, x_sample
    outs_p, outs_s = [], []
    for l in range(depth):
        last = l == depth - 1
        params = dict(
            norm_mix=norm_mix[l][None], w_in=w_in[l].astype(bf16), cw=rnn_conv_w[l], cb=rnn_conv_b[l][None],
            w_gate=_block_diag_gates(w_gate_r[l], w_gate_i[l]), b_r=b_gate_r[l][None], b_i=b_gate_i[l][None],
            lam=rglru_lambda[l][None], dww=dw_w[l], dwb=dw_b[l][None], ln_g=ln_conv_g[l][None],
            ln_b=ln_conv_b[l][None], onr=out_norm_rnn[l][None], onc=out_norm_conv[l][None],
            w_out=w_out[l].astype(bf16), norm_mlp=norm_mlp[l][None], w_up=w_up[l].astype(bf16),
            w_down=w_down[l].astype(bf16), norm_final=norm_final[None])
        f32 = jnp.float32
        zero_state = (jnp.zeros((1, c_hist, d_rnn), f32), jnp.zeros((1, 1, d_rnn), f32),
                      jnp.zeros((1, v_hist, d_conv), f32))
        res_m = _run_layer(xm, *zero_state, params, n_streams=1, tm=xm.shape[1], emit_y=not last,
                           final_norm=False)
        if not last:
            xm, res_m = res_m[0], res_m[1:]
        res_p = _run_layer(xp, *res_m, params, n_streams=1, tm=min(512, xp.shape[1]), emit_y=True,
                           final_norm=last)
        res_s = _run_layer(xs, state_rglru_conv[l], state_rglru_h[l][:, None], state_dwconv[l], params,
                           n_streams=xs.shape[0], tm=xs.shape[1], emit_y=True, final_norm=last)
        xp, xs = res_p[0], res_s[0]
        outs_p.append(res_p[1:])
        outs_s.append(res_s[1:])

    def stack(outs, i):
        return jnp.stack([o[i] for o in outs])

    return (xp, xs,
            stack(outs_p, 0), stack(outs_p, 1)[:, :, 0], stack(outs_p, 2),
            stack(outs_s, 0), stack(outs_s, 1)[:, :, 0], stack(outs_s, 2))
```

```python
import functools
import math

import jax
import jax.numpy as jnp
from jax import lax
from jax.experimental import pallas as pl
from jax.experimental.pallas import tpu as pltpu

EPS = 1e-6
RGLRU_C = 8.0
SUBLANES = 8
LANES = 128
MXU_DIM = 256
VMEM_LIMIT_BYTES = 60000 * 1024


def _rms(x, g):
    ms = jnp.mean(x * x, axis=-1, keepdims=True)
    return x * lax.rsqrt(ms + EPS) * g


def _sigmoid(x):
    return 1.0 / (1.0 + jnp.exp(-x))


def _gelu_tanh(x):
    cdf = 0.5 * (1.0 + jnp.tanh(math.sqrt(2.0 / math.pi) * (x + 0.044715 * (x * x * x))))
    return x * cdf


def _bf16_dot(a, b):
    return jnp.dot(a.astype(jnp.bfloat16), b, preferred_element_type=jnp.float32)


def _to_slabs(dst, idx, x):
    for c in range(x.shape[-1] // LANES):
        dst[idx + (c,)] = x[:, c * LANES:(c + 1) * LANES]


def _from_slabs(buf):
    return jnp.concatenate([buf[c] for c in range(buf.shape[0])], axis=-1)


def _strided_causal_conv(win, s, w_ref, b_ref, out, *, taps, lead, tm, rows):
    half = rows // 2
    for c in range(win.shape[1]):
        lanes = slice(c * LANES, (c + 1) * LANES)
        for t0 in [i * rows + par for i in range(tm // rows) for par in range(2)]:
            acc = win[s, c, pl.ds(lead + t0, half, stride=2), :] * w_ref[0:1, lanes]
            for k in range(1, taps):
                acc = acc + win[s, c, pl.ds(lead + t0 + k, half, stride=2), :] * w_ref[k:k + 1, lanes]
            out[c, pl.ds(s * tm + t0, half, stride=2), :] = acc + b_ref[:, lanes]


def _scan_radices(n_rows):
    n_groups, radices = n_rows // SUBLANES, []
    while n_groups > 1:
        assert n_groups % 2 == 0
        r = 4 if n_groups % 4 == 0 else 2
        radices.append(r)
        n_groups //= r
    return radices


def _scan_level_rows(n_rows):
    rows, out = n_rows, []
    for r in _scan_radices(n_rows)[:-1]:
        rows //= r
        out.append(rows)
    return out


def _linear_scan_slab(a_ref, b_ref, h_ref, ab_lvl, hin_lvl, base, n_rows, carry):
    radices = _scan_radices(n_rows)
    lvl_rows = _scan_level_rows(n_rows)
    ab_off = [sum(lvl_rows[:i]) for i in range(len(lvl_rows))]
    hin_off = [sum(r + 2 * SUBLANES for r in lvl_rows[:i]) for i in range(len(lvl_rows))]

    partial = []
    rows = n_rows
    for lvl, r in enumerate(radices):
        g = rows // r
        src, off = ((a_ref, b_ref), base) if lvl == 0 else ((ab_lvl.at[0], ab_lvl.at[1]), ab_off[lvl - 1])
        ld = lambda kind, k: src[kind][pl.ds(off + k, g, stride=r), :]
        pa, pb = ld(0, 0), ld(1, 0)
        maps = [(pa, pb)]
        for k in range(1, r):
            ak, bk = ld(0, k), ld(1, k)
            pa, pb = ak * pa, ak * pb + bk
            maps.append((pa, pb))
        partial.append(maps)
        rows = g
        if lvl + 1 < len(radices):
            ab_lvl[0, ab_off[lvl]:ab_off[lvl] + g, :] = pa
            ab_lvl[1, ab_off[lvl]:ab_off[lvl] + g, :] = pb
    if not radices:
        pa, pb = a_ref[base:base + SUBLANES, :], b_ref[base:base + SUBLANES, :]

    row = lax.broadcasted_iota(jnp.int32, (SUBLANES, LANES), 0)
    for sh in (1, 2, 4):
        keep = row >= sh
        pa_s = jnp.where(keep, pltpu.roll(pa, sh, 0), 1.0)
        pb_s = jnp.where(keep, pltpu.roll(pb, sh, 0), 0.0)
        pa, pb = pa * pa_s, pa * pb_s + pb
    h_top = pa * carry + pb
    new_carry = jnp.broadcast_to(h_top[SUBLANES - 1:SUBLANES, :], (SUBLANES, LANES))
    if not radices:
        h_ref[base:base + SUBLANES, :] = h_top
        return new_carry
    h_prev = jnp.where(row == 0, carry, pltpu.roll(h_top, 1, 0))

    for lvl in reversed(range(len(radices))):
        r = radices[lvl]
        g = partial[lvl][0][0].shape[0]
        if lvl < len(radices) - 1:
            o = hin_off[lvl] + SUBLANES
            h_prev = hin_lvl[o:o + g, :]
        for k, (pa, pb) in enumerate(partial[lvl]):
            hk = pa * h_prev + pb
            if lvl == 0:
                h_ref[pl.ds(base + k, g, stride=r), :] = hk
            else:
                o = hin_off[lvl - 1] + SUBLANES
                hin_lvl[pl.ds(o + 1 + k, g, stride=r), :] = hk
        if lvl > 0:
            o = hin_off[lvl - 1] + SUBLANES
            hin_lvl[o:o + 1, :] = carry[0:1, :]
    return new_carry


def _layer_kernel(x_ref, conv0_ref, h0_ref, dw0_ref,
                  norm_mix_ref, w_in_ref, cw_ref, cb_ref, w_gate_ref, b_r_ref, b_i_ref, lam_ref,
                  dww_ref, dwb_ref, ln_g_ref, ln_b_ref, onr_ref, onc_ref,
                  w_out_ref, norm_mlp_ref, w_up_ref, w_down_ref, norm_final_ref,
                  *rest, n_streams, tm, emit_y, final_norm, rnn_conv_w, dw_conv_w, ff_chunk, conv_rows):
    if emit_y:
        y_ref, conv_out_ref, h_out_ref, dw_out_ref = rest[:4]
        scratch = rest[4:]
    else:
        conv_out_ref, h_out_ref, dw_out_ref = rest[:3]
        scratch = rest[3:]
    cbuf, vbuf, hcar, a_buf, b_buf, h_buf, ab_lvl, hin_lvl, xc_buf, vc_buf = scratch

    S, TM = n_streams, tm
    M = S * TM
    d_rnn = cw_ref.shape[-1]
    d_conv = dww_ref.shape[-1]
    cpad = cbuf.shape[2] - TM
    vpad = vbuf.shape[2] - TM
    c_hist = rnn_conv_w - 1
    v_hist = dw_conv_w - 1
    t = pl.program_id(1)

    @pl.when(t == 0)
    def _init():
        cbuf[:, :, 0:cpad - c_hist, :] = jnp.zeros((S, d_rnn // LANES, cpad - c_hist, LANES), jnp.float32)
        vbuf[:, :, 0:vpad - v_hist, :] = jnp.zeros((S, d_conv // LANES, vpad - v_hist, LANES), jnp.float32)
        for c in range(d_rnn // LANES):
            cbuf[:, c, cpad - c_hist:cpad, :] = conv0_ref[:, :, c * LANES:(c + 1) * LANES]
        for c in range(d_conv // LANES):
            vbuf[:, c, vpad - v_hist:vpad, :] = dw0_ref[:, :, c * LANES:(c + 1) * LANES]
        hcar[...] = jnp.broadcast_to(h0_ref[...], hcar.shape)

    x = x_ref[...].reshape(M, x_ref.shape[-1])
    hn = _rms(x, norm_mix_ref[...])
    z = _bf16_dot(hn, w_in_ref[...])
    xr = z[:, 0:d_rnn]
    gate = z[:, d_rnn:2 * d_rnn]
    glu_v = z[:, 2 * d_rnn:2 * d_rnn + d_conv]
    glu_g = z[:, 2 * d_rnn + d_conv:]

    for s in range(S):
        _to_slabs(cbuf.at[:, :, cpad:cpad + TM, :], (s,), xr[s * TM:(s + 1) * TM, :])
        _strided_causal_conv(cbuf, s, cw_ref, cb_ref, xc_buf, taps=rnn_conv_w, lead=cpad - c_hist, tm=TM,
                             rows=conv_rows)
        for c in range(d_rnn // LANES):
            conv_out_ref[s, :, c * LANES:(c + 1) * LANES] = cbuf[s, c, cpad + TM - c_hist:cpad + TM, :]
            cbuf[s, c, 0:cpad, :] = cbuf[s, c, TM:TM + cpad, :]
    xr_c = _from_slabs(xc_buf)

    xb = xr_c.astype(jnp.bfloat16)
    r_parts, i_parts = [], []
    for j in range(d_rnn // MXU_DIM):
        ri = jnp.dot(xb[:, j * MXU_DIM:(j + 1) * MXU_DIM], w_gate_ref[j], preferred_element_type=jnp.float32)
        r_parts.append(ri[:, :MXU_DIM])
        i_parts.append(ri[:, MXU_DIM:])
    r = _sigmoid(jnp.concatenate(r_parts, axis=-1) + b_r_ref[...])
    ig = _sigmoid(jnp.concatenate(i_parts, axis=-1) + b_i_ref[...])
    nl = -lam_ref[...]
    softplus_nl = jnp.maximum(nl, 0.0) + jnp.log1p(jnp.exp(-jnp.abs(nl)))
    log_a = (-RGLRU_C * r) * softplus_nl
    a = jnp.exp(log_a)
    mult = jnp.sqrt(jnp.maximum(1.0 - a * a, 0.0))
    _to_slabs(a_buf, (), a)
    _to_slabs(b_buf, (), mult * ig * xr_c)
    for s in range(S):
        carry = hcar[s]
        carry = jnp.concatenate(
            [_linear_scan_slab(a_buf.at[c], b_buf.at[c], h_buf.at[c], ab_lvl.at[:, s, c], hin_lvl.at[s, c],
                               s * TM, TM, carry[:, c * LANES:(c + 1) * LANES])
             for c in range(d_rnn // LANES)], axis=-1)
        hcar[s] = carry
        h_out_ref[s] = carry[0:1, :]

    v = glu_v * _sigmoid(glu_g)
    for s in range(S):
        _to_slabs(vbuf.at[:, :, vpad:vpad + TM, :], (s,), v[s * TM:(s + 1) * TM, :])
        _strided_causal_conv(vbuf, s, dww_ref, dwb_ref, vc_buf, taps=dw_conv_w, lead=vpad - v_hist, tm=TM,
                             rows=conv_rows)
        for c in range(d_conv // LANES):
            dw_out_ref[s, :, c * LANES:(c + 1) * LANES] = vbuf[s, c, vpad + TM - v_hist:vpad + TM, :]
            vbuf[s, c, 0:vpad, :] = vbuf[s, c, TM:TM + vpad, :]

    if not emit_y:
        return

    y_rnn = _from_slabs(h_buf) * _gelu_tanh(gate)
    vc = _from_slabs(vc_buf)
    mu = jnp.mean(vc, axis=-1, keepdims=True)
    vcc = vc - mu
    ln = vcc * lax.rsqrt(jnp.mean(vcc * vcc, axis=-1, keepdims=True) + EPS) * ln_g_ref[...] + ln_b_ref[...]
    y_conv = ln * _sigmoid(ln)
    mix = jnp.concatenate([_rms(y_rnn, onr_ref[...]), _rms(y_conv, onc_ref[...])], axis=-1)
    x1 = x + _bf16_dot(mix, w_out_ref[...])

    hm = _rms(x1, norm_mlp_ref[...]).astype(jnp.bfloat16)
    d_ff = w_up_ref.shape[-1]
    x2 = x1
    for c in range(d_ff // ff_chunk):
        hc = jnp.dot(hm, w_up_ref[:, c * ff_chunk:(c + 1) * ff_chunk], preferred_element_type=jnp.float32)
        hc = jnp.square(jnp.maximum(hc, 0.0))
        x2 = x2 + _bf16_dot(hc, w_down_ref[c * ff_chunk:(c + 1) * ff_chunk, :])
    if final_norm:
        x2 = _rms(x2, norm_final_ref[...])
    y_ref[...] = x2.reshape(y_ref.shape)


def _const_spec(arr):
    nd = arr.ndim
    return pl.BlockSpec(arr.shape, lambda b, t, _nd=nd: (0,) * _nd, pipeline_mode=pl.Buffered(1))


def _run_layer(x, conv0, h0, dw0, params, *, n_streams, tm, emit_y, final_norm):
    B, T, D = x.shape
    S, TM = n_streams, tm
    assert B % S == 0 and T % TM == 0 and TM % SUBLANES == 0
    rnn_conv_w, d_rnn = params["cw"].shape
    dw_conv_w, d_conv = params["dww"].shape
    assert d_rnn % MXU_DIM == 0 and d_conv % LANES == 0
    cpad = -(-(rnn_conv_w - 1) // SUBLANES) * SUBLANES
    vpad = -(-(dw_conv_w - 1) // SUBLANES) * SUBLANES
    conv_rows = min(64, TM)
    assert TM % conv_rows == 0 and conv_rows % (2 * SUBLANES) == 0
    d_ff = params["w_up"].shape[-1]
    ff_chunk = min(1024, d_ff)

    def state_spec(arr):
        blk = (S,) + arr.shape[1:]
        if arr.shape[0] == B:
            return pl.BlockSpec(blk, lambda b, t: (b, 0, 0))
        assert arr.shape[0] == 1 and S == 1
        return pl.BlockSpec(blk, lambda b, t: (0, 0, 0))

    order = ["norm_mix", "w_in", "cw", "cb", "w_gate", "b_r", "b_i", "lam", "dww", "dwb", "ln_g", "ln_b",
             "onr", "onc", "w_out", "norm_mlp", "w_up", "w_down", "norm_final"]
    weights = [params[k] for k in order]
    in_specs = [pl.BlockSpec((S, TM, D), lambda b, t: (b, t, 0)),
                state_spec(conv0), state_spec(h0), state_spec(dw0)] + [_const_spec(w) for w in weights]

    f32 = jnp.float32
    out_shape = [jax.ShapeDtypeStruct((B, rnn_conv_w - 1, d_rnn), f32),
                 jax.ShapeDtypeStruct((B, 1, d_rnn), f32),
                 jax.ShapeDtypeStruct((B, dw_conv_w - 1, d_conv), f32)]
    out_specs = [pl.BlockSpec((S, rnn_conv_w - 1, d_rnn), lambda b, t: (b, 0, 0)),
                 pl.BlockSpec((S, 1, d_rnn), lambda b, t: (b, 0, 0)),
                 pl.BlockSpec((S, dw_conv_w - 1, d_conv), lambda b, t: (b, 0, 0))]
    if emit_y:
        out_shape = [jax.ShapeDtypeStruct((B, T, D), x.dtype)] + out_shape
        out_specs = [pl.BlockSpec((S, TM, D), lambda b, t: (b, t, 0))] + out_specs

    M = S * TM
    lvl_rows = _scan_level_rows(TM)
    scratch_shapes = [
        pltpu.VMEM((S, d_rnn // LANES, cpad + TM, LANES), f32),
        pltpu.VMEM((S, d_conv // LANES, vpad + TM, LANES), f32),
        pltpu.VMEM((S, SUBLANES, d_rnn), f32),
        pltpu.VMEM((d_rnn // LANES, M, LANES), f32),
        pltpu.VMEM((d_rnn // LANES, M, LANES), f32),
        pltpu.VMEM((d_rnn // LANES, M, LANES), f32),
        pltpu.VMEM((2, S, d_rnn // LANES, max(sum(lvl_rows), SUBLANES), LANES), f32),
        pltpu.VMEM((S, d_rnn // LANES, max(sum(r + 2 * SUBLANES for r in lvl_rows), SUBLANES), LANES), f32),
        pltpu.VMEM((d_rnn // LANES, M, LANES), f32),
        pltpu.VMEM((d_conv // LANES, M, LANES), f32),
    ]
    kern = functools.partial(_layer_kernel, n_streams=S, tm=TM, emit_y=emit_y, final_norm=final_norm,
                             rnn_conv_w=rnn_conv_w, dw_conv_w=dw_conv_w, ff_chunk=ff_chunk, conv_rows=conv_rows)
    return pl.pallas_call(
        kern,
        grid=(B // S, T // TM),
        in_specs=in_specs,
        out_specs=out_specs,
        out_shape=out_shape,
        scratch_shapes=scratch_shapes,
        compiler_params=pltpu.CompilerParams(
            dimension_semantics=("arbitrary", "arbitrary"),
            vmem_limit_bytes=VMEM_LIMIT_BYTES),
        name=f"hybrid_layer_s{S}_t{TM}",
    )(x, conv0, h0, dw0, *weights)


def _block_diag_gates(w_r, w_i):
    n_heads, hd, _ = w_r.shape
    per = MXU_DIM // hd
    n_slab = n_heads // per
    on_diag = jnp.eye(per, dtype=bool)[None, :, None, :, None]

    def slabs(w):
        w = w.reshape(n_slab, per, hd, hd)
        bd = jnp.where(on_diag, w[:, :, :, None, :], 0.0)
        return bd.reshape(n_slab, MXU_DIM, MXU_DIM)

    return jnp.concatenate([slabs(w_r), slabs(w_i)], axis=-1).astype(jnp.bfloat16)


def kernel(x_prompt, x_sample, state_rglru_conv, state_rglru_h, state_dwconv, meta_tokens, norm_mix, w_in,
           rnn_conv_w, rnn_conv_b, w_gate_r, b_gate_r, w_gate_i, b_gate_i, rglru_lambda, dw_w, dw_b, ln_conv_g,
           ln_conv_b, out_norm_rnn, out_norm_conv, w_out, norm_mlp, w_up, w_down, norm_final):
    depth = norm_mix.shape[0]
    bp = x_prompt.shape[0]
    bf16 = jnp.bfloat16
    d_rnn = rnn_conv_w.shape[-1]
    d_conv = dw_w.shape[-1]
    c_hist = rnn_conv_w.shape[1] - 1
    v_hist = dw_w.shape[1] - 1

    xm = meta_tokens[None].astype(x_prompt.dtype)
    xp, xs = x_prompt, x_sample
    outs_p, outs_s = [], []
    for l in range(depth):
        last = l == depth - 1
        params = dict(
            norm_mix=norm_mix[l][None], w_in=w_in[l].astype(bf16), cw=rnn_conv_w[l], cb=rnn_conv_b[l][None],
            w_gate=_block_diag_gates(w_gate_r[l], w_gate_i[l]), b_r=b_gate_r[l][None], b_i=b_gate_i[l][None],
            lam=rglru_lambda[l][None], dww=dw_w[l], dwb=dw_b[l][None], ln_g=ln_conv_g[l][None],
            ln_b=ln_conv_b[l][None], onr=out_norm_rnn[l][None], onc=out_norm_conv[l][None],
            w_out=w_out[l].astype(bf16), norm_mlp=norm_mlp[l][None], w_up=w_up[l].astype(bf16),
            w_down=w_down[l].astype(bf16), norm_final=norm_final[None])
        f32 = jnp.float32
        zero_state = (jnp.zeros((1, c_hist, d_rnn), f32), jnp.zeros((1, 1, d_rnn), f32),
                      jnp.zeros((1, v_hist, d_conv), f32))
        res_m = _run_layer(xm, *zero_state, params, n_streams=1, tm=xm.shape[1], emit_y=not last,
                           final_norm=False)
        if not last:
            xm, res_m = res_m[0], res_m[1:]
        res_p = _run_layer(xp, *res_m, params, n_streams=1, tm=min(512, xp.shape[1]), emit_y=True,
                           final_norm=last)
        res_s = _run_layer(xs, state_rglru_conv[l], state_rglru_h[l][:, None], state_dwconv[l], params,
                           n_streams=xs.shape[0], tm=xs.shape[1], emit_y=True, final_norm=last)
        xp, xs = res_p[0], res_s[0]
        outs_p.append(res_p[1:])
        outs_s.append(res_s[1:])

    def stack(outs, i):
        return jnp.stack([o[i] for o in outs])

    return (xp, xs,
            stack(outs_p, 0), stack(outs_p, 1)[:, :, 0], stack(outs_p, 2),
            stack(outs_s, 0), stack(outs_s, 1)[:, :, 0], stack(outs_s, 2))
```

```python
import functools
import math

import jax
import jax.numpy as jnp
from jax import lax
from jax.experimental import pallas as pl
from jax.experimental.pallas import tpu as pltpu

EPS = 1e-6
RGLRU_C = 8.0
SUBLANES = 8
LANES = 128
MXU_DIM = 256
VMEM_LIMIT_BYTES = 60000 * 1024


def _rms(x, g):
    ms = jnp.mean(x * x, axis=-1, keepdims=True)
    return x * lax.rsqrt(ms + EPS) * g


def _sigmoid(x):
    return 1.0 / (1.0 + jnp.exp(-x))


def _gelu_tanh(x):
    cdf = 0.5 * (1.0 + jnp.tanh(math.sqrt(2.0 / math.pi) * (x + 0.044715 * (x * x * x))))
    return x * cdf


def _bf16_dot(a, b):
    return jnp.dot(a.astype(jnp.bfloat16), b, preferred_element_type=jnp.float32)


def _to_slabs(dst, idx, x):
    for c in range(x.shape[-1] // LANES):
        dst[idx + (c,)] = x[:, c * LANES:(c + 1) * LANES]


def _from_slabs(buf):
    return jnp.concatenate([buf[c] for c in range(buf.shape[0])], axis=-1)


def _strided_causal_conv(win, s, w_ref, b_ref, out, *, taps, lead, tm, rows):
    half = rows // 2
    for c in range(win.shape[1]):
        lanes = slice(c * LANES, (c + 1) * LANES)
        for t0 in [i * rows + par for i in range(tm // rows) for par in range(2)]:
            acc = win[s, c, pl.ds(lead + t0, half, stride=2), :] * w_ref[0:1, lanes]
            for k in range(1, taps):
                acc = acc + win[s, c, pl.ds(lead + t0 + k, half, stride=2), :] * w_ref[k:k + 1, lanes]
            out[c, pl.ds(s * tm + t0, half, stride=2), :] = acc + b_ref[:, lanes]


def _scan_radices(n_rows):
    n_groups, radices = n_rows // SUBLANES, []
    while n_groups > 1:
        assert n_groups % 2 == 0
        r = 4 if n_groups % 4 == 0 else 2
        radices.append(r)
        n_groups //= r
    return radices


def _scan_level_rows(n_rows):
    rows, out = n_rows, []
    for r in _scan_radices(n_rows)[:-1]:
        rows //= r
        out.append(rows)
    return out


def _linear_scan_slab(a_ref, b_ref, h_ref, ab_lvl, hin_lvl, base, n_rows, carry):
    radices = _scan_radices(n_rows)
    lvl_rows = _scan_level_rows(n_rows)
    ab_off = [sum(lvl_rows[:i]) for i in range(len(lvl_rows))]
    hin_off = [sum(r + 2 * SUBLANES for r in lvl_rows[:i]) for i in range(len(lvl_rows))]

    partial = []
    rows = n_rows
    for lvl, r in enumerate(radices):
        g = rows // r
        src, off = ((a_ref, b_ref), base) if lvl == 0 else ((ab_lvl.at[0], ab_lvl.at[1]), ab_off[lvl - 1])
        ld = lambda kind, k: src[kind][pl.ds(off + k, g, stride=r), :]
        pa, pb = ld(0, 0), ld(1, 0)
        maps = [(pa, pb)]
        for k in range(1, r):
            ak, bk = ld(0, k), ld(1, k)
            pa, pb = ak * pa, ak * pb + bk
            maps.append((pa, pb))
        partial.append(maps)
        rows = g
        if lvl + 1 < len(radices):
            ab_lvl[0, ab_off[lvl]:ab_off[lvl] + g, :] = pa
            ab_lvl[1, ab_off[lvl]:ab_off[lvl] + g, :] = pb
    if not radices:
        pa, pb = a_ref[base:base + SUBLANES, :], b_ref[base:base + SUBLANES, :]

    row = lax.broadcasted_iota(jnp.int32, (SUBLANES, LANES), 0)
    for sh in (1, 2, 4):
        keep = row >= sh
        pa_s = jnp.where(keep, pltpu.roll(pa, sh, 0), 1.0)
        pb_s = jnp.where(keep, pltpu.roll(pb, sh, 0), 0.0)
        pa, pb = pa * pa_s, pa * pb_s + pb
    h_top = pa * carry + pb
    new_carry = jnp.broadcast_to(h_top[SUBLANES - 1:SUBLANES, :], (SUBLANES, LANES))
    if not radices:
        h_ref[base:base + SUBLANES, :] = h_top
        return new_carry
    h_prev = jnp.where(row == 0, carry, pltpu.roll(h_top, 1, 0))

    for lvl in reversed(range(len(radices))):
        r = radices[lvl]
        g = partial[lvl][0][0].shape[0]
        if lvl < len(radices) - 1:
            o = hin_off[lvl] + SUBLANES
            h_prev = hin_lvl[o:o + g, :]
        for k, (pa, pb) in enumerate(partial[lvl]):
            hk = pa * h_prev + pb
            if lvl == 0:
                h_ref[pl.ds(base + k, g, stride=r), :] = hk
            else:
                o = hin_off[lvl - 1] + SUBLANES
                hin_lvl[pl.ds(o + 1 + k, g, stride=r), :] = hk
        if lvl > 0:
            o = hin_off[lvl - 1] + SUBLANES
            hin_lvl[o:o + 1, :] = carry[0:1, :]
    return new_carry


def _layer_kernel(x_ref, conv0_ref, h0_ref, dw0_ref,
                  norm_mix_ref, w_in_ref, cw_ref, cb_ref, w_gate_ref, b_r_ref, b_i_ref, lam_ref,
                  dww_ref, dwb_ref, ln_g_ref, ln_b_ref, onr_ref, onc_ref,
                  w_out_ref, norm_mlp_ref, w_up_ref, w_down_ref, norm_final_ref,
                  *rest, n_streams, tm, n_tiles, tiles_per_stream, emit_y, final_norm, pipelined,
                  rnn_conv_w, dw_conv_w, ff_chunk, conv_rows):
    if emit_y:
        y_ref, conv_out_ref, h_out_ref, dw_out_ref = rest[:4]
        scratch = rest[4:]
    else:
        conv_out_ref, h_out_ref, dw_out_ref = rest[:3]
        scratch = rest[3:]
    cbuf, vbuf, hcar, a_buf, b_buf, h_buf, ab_lvl, hin_lvl, xc_buf, vc_buf = scratch[:10]

    S, TM = n_streams, tm
    M = S * TM
    d_rnn = cw_ref.shape[-1]
    d_conv = dww_ref.shape[-1]
    cpad = cbuf.shape[2] - TM
    vpad = vbuf.shape[2] - TM
    c_hist = rnn_conv_w - 1
    v_hist = dw_conv_w - 1
    n = pl.program_id(0)
    tile = jnp.minimum(n, n_tiles - 1)
    live = n < n_tiles

    @pl.when(jnp.logical_and(lax.rem(tile, tiles_per_stream) == 0, live))
    def _load_state():
        cbuf[:, :, 0:cpad - c_hist, :] = jnp.zeros((S, d_rnn // LANES, cpad - c_hist, LANES), jnp.float32)
        vbuf[:, :, 0:vpad - v_hist, :] = jnp.zeros((S, d_conv // LANES, vpad - v_hist, LANES), jnp.float32)
        for c in range(d_rnn // LANES):
            cbuf[:, c, cpad - c_hist:cpad, :] = conv0_ref[:, :, c * LANES:(c + 1) * LANES]
        for c in range(d_conv // LANES):
            vbuf[:, c, vpad - v_hist:vpad, :] = dw0_ref[:, :, c * LANES:(c + 1) * LANES]
        hcar[...] = jnp.broadcast_to(h0_ref[...], hcar.shape)

    def in_proj():
        x = x_ref[...].reshape(M, x_ref.shape[-1])
        hn = _rms(x, norm_mix_ref[...])
        return _bf16_dot(hn, w_in_ref[...])

    def gate_proj(z):
        xr = z[:, 0:d_rnn]
        for s in range(S):
            _to_slabs(cbuf.at[:, :, cpad:cpad + TM, :], (s,), xr[s * TM:(s + 1) * TM, :])
            _strided_causal_conv(cbuf, s, cw_ref, cb_ref, xc_buf, taps=rnn_conv_w, lead=cpad - c_hist, tm=TM,
                                 rows=conv_rows)
            for c in range(d_rnn // LANES):
                cbuf[s, c, 0:cpad, :] = cbuf[s, c, TM:TM + cpad, :]
        xr_c = _from_slabs(xc_buf)

        xb = xr_c.astype(jnp.bfloat16)
        r_parts, i_parts = [], []
        for j in range(d_rnn // MXU_DIM):
            ri = jnp.dot(xb[:, j * MXU_DIM:(j + 1) * MXU_DIM], w_gate_ref[j], preferred_element_type=jnp.float32)
            r_parts.append(ri[:, :MXU_DIM])
            i_parts.append(ri[:, MXU_DIM:])
        return xr_c, r_parts, i_parts

    def mixer_rest(z, xr_c, r_parts, i_parts):
        x = x_ref[...].reshape(M, x_ref.shape[-1])
        gate = z[:, d_rnn:2 * d_rnn]
        glu_v = z[:, 2 * d_rnn:2 * d_rnn + d_conv]
        glu_g = z[:, 2 * d_rnn + d_conv:]
        r = _sigmoid(jnp.concatenate(r_parts, axis=-1) + b_r_ref[...])
        ig = _sigmoid(jnp.concatenate(i_parts, axis=-1) + b_i_ref[...])
        nl = -lam_ref[...]
        softplus_nl = jnp.maximum(nl, 0.0) + jnp.log1p(jnp.exp(-jnp.abs(nl)))
        log_a = (-RGLRU_C * r) * softplus_nl
        a = jnp.exp(log_a)
        mult = jnp.sqrt(jnp.maximum(1.0 - a * a, 0.0))
        _to_slabs(a_buf, (), a)
        _to_slabs(b_buf, (), mult * ig * xr_c)
        for s in range(S):
            carry = hcar[s]
            hcar[s] = jnp.concatenate(
                [_linear_scan_slab(a_buf.at[c], b_buf.at[c], h_buf.at[c], ab_lvl.at[:, s, c], hin_lvl.at[s, c],
                                   s * TM, TM, carry[:, c * LANES:(c + 1) * LANES])
                 for c in range(d_rnn // LANES)], axis=-1)

        v = glu_v * _sigmoid(glu_g)
        for s in range(S):
            _to_slabs(vbuf.at[:, :, vpad:vpad + TM, :], (s,), v[s * TM:(s + 1) * TM, :])
            _strided_causal_conv(vbuf, s, dww_ref, dwb_ref, vc_buf, taps=dw_conv_w, lead=vpad - v_hist, tm=TM,
                                 rows=conv_rows)
            for c in range(d_conv // LANES):
                vbuf[s, c, 0:vpad, :] = vbuf[s, c, TM:TM + vpad, :]
        if not emit_y:
            return None

        y_rnn = _from_slabs(h_buf) * _gelu_tanh(gate)
        vc = _from_slabs(vc_buf)
        mu = jnp.mean(vc, axis=-1, keepdims=True)
        vcc = vc - mu
        ln = vcc * lax.rsqrt(jnp.mean(vcc * vcc, axis=-1, keepdims=True) + EPS) * ln_g_ref[...] + ln_b_ref[...]
        y_conv = ln * _sigmoid(ln)
        mix = jnp.concatenate([_rms(y_rnn, onr_ref[...]), _rms(y_conv, onc_ref[...])], axis=-1)
        return x + _bf16_dot(mix, w_out_ref[...])

    n_ff = w_up_ref.shape[-1] // ff_chunk

    def mlp_chunk(hm, x2, c):
        hc = jnp.dot(hm, w_up_ref[:, c * ff_chunk:(c + 1) * ff_chunk], preferred_element_type=jnp.float32)
        hc = jnp.square(jnp.maximum(hc, 0.0))
        return x2 + _bf16_dot(hc, w_down_ref[c * ff_chunk:(c + 1) * ff_chunk, :])

    def mlp_begin(x1, n_chunks):
        hm = _rms(x1, norm_mlp_ref[...]).astype(jnp.bfloat16)
        x2 = x1
        for c in range(n_chunks):
            x2 = mlp_chunk(hm, x2, c)
        return hm, x2, n_chunks

    def mlp_finish(hm, x2, n_done):
        for c in range(n_done, n_ff):
            x2 = mlp_chunk(hm, x2, c)
        if final_norm:
            x2 = _rms(x2, norm_final_ref[...])
        y_ref[...] = x2.reshape(y_ref.shape)

    def store_state():
        for s in range(S):
            for c in range(d_rnn // LANES):
                conv_out_ref[s, :, c * LANES:(c + 1) * LANES] = cbuf[s, c, cpad - c_hist:cpad, :]
            for c in range(d_conv // LANES):
                dw_out_ref[s, :, c * LANES:(c + 1) * LANES] = vbuf[s, c, vpad - v_hist:vpad, :]
            h_out_ref[s] = hcar[s, 0:1, :]

    if pipelined:
        x1_buf = scratch[10]

        @pl.when(n == 0)
        def _no_previous_tile():
            x1_buf[...] = jnp.zeros(x1_buf.shape, x1_buf.dtype)

        z = in_proj()
        mlp_state = mlp_begin(x1_buf[...], n_ff - 1)
        gates = gate_proj(z)
        mlp_finish(*mlp_state)
        x1_buf[...] = mixer_rest(z, *gates)
        pl.when(live)(store_state)
    else:
        z = in_proj()
        x1 = mixer_rest(z, *gate_proj(z))
        if emit_y:
            mlp_finish(*mlp_begin(x1, 0))
        store_state()


def _const_spec(arr):
    nd = arr.ndim
    return pl.BlockSpec(arr.shape, lambda n, _nd=nd: (0,) * _nd, pipeline_mode=pl.Buffered(1))


def _run_layer(x, conv0, h0, dw0, params, *, n_streams, tm, emit_y, final_norm):
    B, T, D = x.shape
    S, TM = n_streams, tm
    assert B % S == 0 and T % TM == 0 and TM % SUBLANES == 0
    rnn_conv_w, d_rnn = params["cw"].shape
    dw_conv_w, d_conv = params["dww"].shape
    assert d_rnn % MXU_DIM == 0 and d_conv % LANES == 0
    cpad = -(-(rnn_conv_w - 1) // SUBLANES) * SUBLANES
    vpad = -(-(dw_conv_w - 1) // SUBLANES) * SUBLANES
    conv_rows = min(64, TM)
    assert TM % conv_rows == 0 and conv_rows % (2 * SUBLANES) == 0
    d_ff = params["w_up"].shape[-1]
    ff_chunk = min(1024, d_ff)
    tps = T // TM
    n_tiles = (B // S) * tps
    pipelined = emit_y and n_tiles > 1

    def in_tile(n):
        return jnp.minimum(n, n_tiles - 1) if pipelined else n

    def out_tile(n):
        return jnp.maximum(n - 1, 0) if pipelined else n

    def state_spec(arr):
        blk = (S,) + arr.shape[1:]
        if arr.shape[0] == B:
            return pl.BlockSpec(blk, lambda n: (in_tile(n) // tps, 0, 0))
        assert arr.shape[0] == 1 and S == 1
        return pl.BlockSpec(blk, lambda n: (0, 0, 0))

    order = ["norm_mix", "w_in", "cw", "cb", "w_gate", "b_r", "b_i", "lam", "dww", "dwb", "ln_g", "ln_b",
             "onr", "onc", "w_out", "norm_mlp", "w_up", "w_down", "norm_final"]
    weights = [params[k] for k in order]
    in_specs = [pl.BlockSpec((S, TM, D), lambda n: (in_tile(n) // tps, in_tile(n) % tps, 0)),
                state_spec(conv0), state_spec(h0), state_spec(dw0)] + [_const_spec(w) for w in weights]

    f32 = jnp.float32
    out_shape = [jax.ShapeDtypeStruct((B, rnn_conv_w - 1, d_rnn), f32),
                 jax.ShapeDtypeStruct((B, 1, d_rnn), f32),
                 jax.ShapeDtypeStruct((B, dw_conv_w - 1, d_conv), f32)]
    out_specs = [pl.BlockSpec((S, rnn_conv_w - 1, d_rnn), lambda n: (in_tile(n) // tps, 0, 0)),
                 pl.BlockSpec((S, 1, d_rnn), lambda n: (in_tile(n) // tps, 0, 0)),
                 pl.BlockSpec((S, dw_conv_w - 1, d_conv), lambda n: (in_tile(n) // tps, 0, 0))]
    if emit_y:
        out_shape = [jax.ShapeDtypeStruct((B, T, D), x.dtype)] + out_shape
        out_specs = [pl.BlockSpec((S, TM, D), lambda n: (out_tile(n) // tps, out_tile(n) % tps, 0))] + out_specs

    M = S * TM
    lvl_rows = _scan_level_rows(TM)
    scratch_shapes = [
        pltpu.VMEM((S, d_rnn // LANES, cpad + TM, LANES), f32),
        pltpu.VMEM((S, d_conv // LANES, vpad + TM, LANES), f32),
        pltpu.VMEM((S, SUBLANES, d_rnn), f32),
        pltpu.VMEM((d_rnn // LANES, M, LANES), f32),
        pltpu.VMEM((d_rnn // LANES, M, LANES), f32),
        pltpu.VMEM((d_rnn // LANES, M, LANES), f32),
        pltpu.VMEM((2, S, d_rnn // LANES, max(sum(lvl_rows), SUBLANES), LANES), f32),
        pltpu.VMEM((S, d_rnn // LANES, max(sum(r + 2 * SUBLANES for r in lvl_rows), SUBLANES), LANES), f32),
        pltpu.VMEM((d_rnn // LANES, M, LANES), f32),
        pltpu.VMEM((d_conv // LANES, M, LANES), f32),
    ]
    if pipelined:
        scratch_shapes.append(pltpu.VMEM((M, D), f32))
    kern = functools.partial(_layer_kernel, n_streams=S, tm=TM, n_tiles=n_tiles, tiles_per_stream=tps,
                             emit_y=emit_y, final_norm=final_norm, pipelined=pipelined,
                             rnn_conv_w=rnn_conv_w, dw_conv_w=dw_conv_w, ff_chunk=ff_chunk, conv_rows=conv_rows)
    return pl.pallas_call(
        kern,
        grid=(n_tiles + (1 if pipelined else 0),),
        in_specs=in_specs,
        out_specs=out_specs,
        out_shape=out_shape,
        scratch_shapes=scratch_shapes,
        compiler_params=pltpu.CompilerParams(
            dimension_semantics=("arbitrary",),
            vmem_limit_bytes=VMEM_LIMIT_BYTES),
        name=f"hybrid_layer_s{S}_t{TM}",
    )(x, conv0, h0, dw0, *weights)


def _block_diag_gates(w_r, w_i):
    n_heads, hd, _ = w_r.shape
    per = MXU_DIM // hd
    n_slab = n_heads // per
    on_diag = jnp.eye(per, dtype=bool)[None, :, None, :, None]

    def slabs(w):
        w = w.reshape(n_slab, per, hd, hd)
        bd = jnp.where(on_diag, w[:, :, :, None, :], 0.0)
        return bd.reshape(n_slab, MXU_DIM, MXU_DIM)

    return jnp.concatenate([slabs(w_r), slabs(w_i)], axis=-1).astype(jnp.bfloat16)


def kernel(x_prompt, x_sample, state_rglru_conv, state_rglru_h, state_dwconv, meta_tokens, norm_mix, w_in,
           rnn_conv_w, rnn_conv_b, w_gate_r, b_gate_r, w_gate_i, b_gate_i, rglru_lambda, dw_w, dw_b, ln_conv_g,
           ln_conv_b, out_norm_rnn, out_norm_conv, w_out, norm_mlp, w_up, w_down, norm_final):
    depth = norm_mix.shape[0]
    bf16 = jnp.bfloat16
    d_rnn = rnn_conv_w.shape[-1]
    d_conv = dw_w.shape[-1]
    c_hist = rnn_conv_w.shape[1] - 1
    v_hist = dw_w.shape[1] - 1

    xm = meta_tokens[None].astype(x_prompt.dtype)
    xp, xs = x_prompt, x_sample
    outs_p, outs_s = [], []
    for l in range(depth):
        last = l == depth - 1
        params = dict(
            norm_mix=norm_mix[l][None], w_in=w_in[l].astype(bf16), cw=rnn_conv_w[l], cb=rnn_conv_b[l][None],
            w_gate=_block_diag_gates(w_gate_r[l], w_gate_i[l]), b_r=b_gate_r[l][None], b_i=b_gate_i[l][None],
            lam=rglru_lambda[l][None], dww=dw_w[l], dwb=dw_b[l][None], ln_g=ln_conv_g[l][None],
            ln_b=ln_conv_b[l][None], onr=out_norm_rnn[l][None], onc=out_norm_conv[l][None],
            w_out=w_out[l].astype(bf16), norm_mlp=norm_mlp[l][None], w_up=w_up[l].astype(bf16),
            w_down=w_down[l].astype(bf16), norm_final=norm_final[None])
        f32 = jnp.float32
        zero_state = (jnp.zeros((1, c_hist, d_rnn), f32), jnp.zeros((1, 1, d_rnn), f32),
                      jnp.zeros((1, v_hist, d_conv), f32))
        res_m = _run_layer(xm, *zero_state, params, n_streams=1, tm=xm.shape[1], emit_y=not last,
                           final_norm=False)
        if not last:
            xm, res_m = res_m[0], res_m[1:]
        res_p = _run_layer(xp, *res_m, params, n_streams=1, tm=min(512, xp.shape[1]), emit_y=True,
                           final_norm=last)
        res_s = _run_layer(xs, state_rglru_conv[l], state_rglru_h[l][:, None], state_dwconv[l], params,
                           n_streams=xs.shape[0], tm=xs.shape[1], emit_y=True, final_norm=last)
        xp, xs = res_p[0], res_s[0]
        outs_p.append(res_p[1:])
        outs_s.append(res_s[1:])

    def stack(outs, i):
        return jnp.stack([o[i] for o in outs])

    return (xp, xs,
            stack(outs_p, 0), stack(outs_p, 1)[:, :, 0], stack(outs_p, 2),
            stack(outs_s, 0), stack(outs_s, 1)[:, :, 0], stack(outs_s, 2))
```

```python
import functools
import math

import jax
import jax.numpy as jnp
from jax import lax
from jax.experimental import pallas as pl
from jax.experimental.pallas import tpu as pltpu

EPS = 1e-6
RGLRU_C = 8.0
SUBLANES = 8
LANES = 128
MXU_DIM = 256
VMEM_LIMIT_BYTES = 60000 * 1024


def _rms(x, g):
    ms = jnp.mean(x * x, axis=-1, keepdims=True)
    return x * lax.rsqrt(ms + EPS) * g


def _sigmoid(x):
    return 1.0 / (1.0 + jnp.exp(-x))


def _gelu_tanh(x):
    cdf = 0.5 * (1.0 + jnp.tanh(math.sqrt(2.0 / math.pi) * (x + 0.044715 * (x * x * x))))
    return x * cdf


def _bf16_dot(a, b):
    return jnp.dot(a.astype(jnp.bfloat16), b, preferred_element_type=jnp.float32)


def _to_slabs(dst, idx, x):
    for c in range(x.shape[-1] // LANES):
        dst[idx + (c,)] = x[:, c * LANES:(c + 1) * LANES]


def _from_slabs(buf):
    return jnp.concatenate([buf[c] for c in range(buf.shape[0])], axis=-1)


def _strided_causal_conv(win, s, w_ref, b_ref, out, *, taps, lead, tm, rows):
    half = rows // 2
    for c in range(win.shape[1]):
        lanes = slice(c * LANES, (c + 1) * LANES)
        for t0 in [i * rows + par for i in range(tm // rows) for par in range(2)]:
            acc = win[s, c, pl.ds(lead + t0, half, stride=2), :] * w_ref[0:1, lanes]
            for k in range(1, taps):
                acc = acc + win[s, c, pl.ds(lead + t0 + k, half, stride=2), :] * w_ref[k:k + 1, lanes]
            out[c, pl.ds(s * tm + t0, half, stride=2), :] = acc + b_ref[:, lanes]


def _scan_radices(n_rows):
    n_groups, radices = n_rows // SUBLANES, []
    while n_groups > 1:
        assert n_groups % 2 == 0
        r = 4 if n_groups % 4 == 0 else 2
        radices.append(r)
        n_groups //= r
    return radices


def _scan_level_rows(n_rows):
    rows, out = n_rows, []
    for r in _scan_radices(n_rows)[:-1]:
        rows //= r
        out.append(rows)
    return out


def _linear_scan_slab(a_ref, b_ref, h_ref, ab_lvl, hin_lvl, base, n_rows, carry):
    radices = _scan_radices(n_rows)
    lvl_rows = _scan_level_rows(n_rows)
    ab_off = [sum(lvl_rows[:i]) for i in range(len(lvl_rows))]
    hin_off = [sum(r + 2 * SUBLANES for r in lvl_rows[:i]) for i in range(len(lvl_rows))]

    partial = []
    rows = n_rows
    for lvl, r in enumerate(radices):
        g = rows // r
        src, off = ((a_ref, b_ref), base) if lvl == 0 else ((ab_lvl.at[0], ab_lvl.at[1]), ab_off[lvl - 1])
        ld = lambda kind, k: src[kind][pl.ds(off + k, g, stride=r), :]
        pa, pb = ld(0, 0), ld(1, 0)
        maps = [(pa, pb)]
        for k in range(1, r):
            ak, bk = ld(0, k), ld(1, k)
            pa, pb = ak * pa, ak * pb + bk
            maps.append((pa, pb))
        partial.append(maps)
        rows = g
        if lvl + 1 < len(radices):
            ab_lvl[0, ab_off[lvl]:ab_off[lvl] + g, :] = pa
            ab_lvl[1, ab_off[lvl]:ab_off[lvl] + g, :] = pb
    if not radices:
        pa, pb = a_ref[base:base + SUBLANES, :], b_ref[base:base + SUBLANES, :]

    row = lax.broadcasted_iota(jnp.int32, (SUBLANES, LANES), 0)
    for sh in (1, 2, 4):
        keep = row >= sh
        pa_s = jnp.where(keep, pltpu.roll(pa, sh, 0), 1.0)
        pb_s = jnp.where(keep, pltpu.roll(pb, sh, 0), 0.0)
        pa, pb = pa * pa_s, pa * pb_s + pb
    h_top = pa * carry + pb
    new_carry = jnp.broadcast_to(h_top[SUBLANES - 1:SUBLANES, :], (SUBLANES, LANES))
    if not radices:
        h_ref[base:base + SUBLANES, :] = h_top
        return new_carry
    h_prev = jnp.where(row == 0, carry, pltpu.roll(h_top, 1, 0))

    for lvl in reversed(range(len(radices))):
        r = radices[lvl]
        g = partial[lvl][0][0].shape[0]
        if lvl < len(radices) - 1:
            o = hin_off[lvl] + SUBLANES
            h_prev = hin_lvl[o:o + g, :]
        for k, (pa, pb) in enumerate(partial[lvl]):
            hk = pa * h_prev + pb
            if lvl == 0:
                h_ref[pl.ds(base + k, g, stride=r), :] = hk
            else:
                o = hin_off[lvl - 1] + SUBLANES
                hin_lvl[pl.ds(o + 1 + k, g, stride=r), :] = hk
        if lvl > 0:
            o = hin_off[lvl - 1] + SUBLANES
            hin_lvl[o:o + 1, :] = carry[0:1, :]
    return new_carry


def _layer_kernel(x_ref, conv0_ref, h0_ref, dw0_ref,
                  norm_mix_ref, w_in_ref, cw_ref, cb_ref, w_gate_ref, b_r_ref, b_i_ref, lam_ref,
                  dww_ref, dwb_ref, ln_g_ref, ln_b_ref, onr_ref, onc_ref,
                  w_out_ref, norm_mlp_ref, w_up_ref, w_down_ref, norm_final_ref,
                  *rest, n_streams, tm, n_tiles, tiles_per_stream, emit_y, final_norm, pipelined,
                  rnn_conv_w, dw_conv_w, ff_chunk, conv_rows):
    if emit_y:
        y_ref, conv_out_ref, h_out_ref, dw_out_ref = rest[:4]
        scratch = rest[4:]
    else:
        conv_out_ref, h_out_ref, dw_out_ref = rest[:3]
        scratch = rest[3:]
    cbuf, vbuf, hcar, a_buf, b_buf, h_buf, ab_lvl, hin_lvl, xc_buf, vc_buf = scratch[:10]

    S, TM = n_streams, tm
    M = S * TM
    d_rnn = cw_ref.shape[-1]
    d_conv = dww_ref.shape[-1]
    cpad = cbuf.shape[2] - TM
    vpad = vbuf.shape[2] - TM
    c_hist = rnn_conv_w - 1
    v_hist = dw_conv_w - 1
    n = pl.program_id(0)
    tile = jnp.minimum(n, n_tiles - 1)
    live = n < n_tiles

    @pl.when(jnp.logical_and(lax.rem(tile, tiles_per_stream) == 0, live))
    def _load_state():
        cbuf[:, :, 0:cpad - c_hist, :] = jnp.zeros((S, d_rnn // LANES, cpad - c_hist, LANES), jnp.float32)
        vbuf[:, :, 0:vpad - v_hist, :] = jnp.zeros((S, d_conv // LANES, vpad - v_hist, LANES), jnp.float32)
        for c in range(d_rnn // LANES):
            cbuf[:, c, cpad - c_hist:cpad, :] = conv0_ref[:, :, c * LANES:(c + 1) * LANES]
        for c in range(d_conv // LANES):
            vbuf[:, c, vpad - v_hist:vpad, :] = dw0_ref[:, :, c * LANES:(c + 1) * LANES]
        hcar[...] = jnp.broadcast_to(h0_ref[...], hcar.shape)

    def in_proj():
        x = x_ref[...].reshape(M, x_ref.shape[-1])
        hn = _rms(x, norm_mix_ref[...])
        return _bf16_dot(hn, w_in_ref[...])

    def gate_proj(z):
        xr = z[:, 0:d_rnn]
        for s in range(S):
            _to_slabs(cbuf.at[:, :, cpad:cpad + TM, :], (s,), xr[s * TM:(s + 1) * TM, :])
            _strided_causal_conv(cbuf, s, cw_ref, cb_ref, xc_buf, taps=rnn_conv_w, lead=cpad - c_hist, tm=TM,
                                 rows=conv_rows)
            for c in range(d_rnn // LANES):
                cbuf[s, c, 0:cpad, :] = cbuf[s, c, TM:TM + cpad, :]
        xr_c = _from_slabs(xc_buf)

        xb = xr_c.astype(jnp.bfloat16)
        r_parts, i_parts = [], []
        for j in range(d_rnn // MXU_DIM):
            ri = jnp.dot(xb[:, j * MXU_DIM:(j + 1) * MXU_DIM], w_gate_ref[j], preferred_element_type=jnp.float32)
            r_parts.append(ri[:, :MXU_DIM])
            i_parts.append(ri[:, MXU_DIM:])
        return xr_c, r_parts, i_parts

    def mixer_rest(z, xr_c, r_parts, i_parts):
        x = x_ref[...].reshape(M, x_ref.shape[-1])
        gate = z[:, d_rnn:2 * d_rnn]
        glu_v = z[:, 2 * d_rnn:2 * d_rnn + d_conv]
        glu_g = z[:, 2 * d_rnn + d_conv:]
        r = _sigmoid(jnp.concatenate(r_parts, axis=-1) + b_r_ref[...])
        ig = _sigmoid(jnp.concatenate(i_parts, axis=-1) + b_i_ref[...])
        nl = -lam_ref[...]
        softplus_nl = jnp.maximum(nl, 0.0) + jnp.log1p(jnp.exp(-jnp.abs(nl)))
        log_a = (-RGLRU_C * r) * softplus_nl
        a = jnp.exp(log_a)
        mult = jnp.sqrt(jnp.maximum(1.0 - a * a, 0.0))
        _to_slabs(a_buf, (), a)
        _to_slabs(b_buf, (), mult * ig * xr_c)
        for s in range(S):
            carry = hcar[s]
            hcar[s] = jnp.concatenate(
                [_linear_scan_slab(a_buf.at[c], b_buf.at[c], h_buf.at[c], ab_lvl.at[:, s, c], hin_lvl.at[s, c],
                                   s * TM, TM, carry[:, c * LANES:(c + 1) * LANES])
                 for c in range(d_rnn // LANES)], axis=-1)

        v = glu_v * _sigmoid(glu_g)
        for s in range(S):
            _to_slabs(vbuf.at[:, :, vpad:vpad + TM, :], (s,), v[s * TM:(s + 1) * TM, :])
            _strided_causal_conv(vbuf, s, dww_ref, dwb_ref, vc_buf, taps=dw_conv_w, lead=vpad - v_hist, tm=TM,
                                 rows=conv_rows)
            for c in range(d_conv // LANES):
                vbuf[s, c, 0:vpad, :] = vbuf[s, c, TM:TM + vpad, :]
        if not emit_y:
            return None

        y_rnn = _from_slabs(h_buf) * _gelu_tanh(gate)
        vc = _from_slabs(vc_buf)
        mu = jnp.mean(vc, axis=-1, keepdims=True)
        vcc = vc - mu
        ln = vcc * lax.rsqrt(jnp.mean(vcc * vcc, axis=-1, keepdims=True) + EPS) * ln_g_ref[...] + ln_b_ref[...]
        y_conv = ln * _sigmoid(ln)
        mix = jnp.concatenate([_rms(y_rnn, onr_ref[...]), _rms(y_conv, onc_ref[...])], axis=-1)
        return x + _bf16_dot(mix, w_out_ref[...])

    n_ff = w_up_ref.shape[-1] // ff_chunk

    def mlp_chunk(hm, x2, c):
        hc = jnp.dot(hm, w_up_ref[:, c * ff_chunk:(c + 1) * ff_chunk], preferred_element_type=jnp.float32)
        hc = jnp.square(jnp.maximum(hc, 0.0))
        return x2 + _bf16_dot(hc, w_down_ref[c * ff_chunk:(c + 1) * ff_chunk, :])

    def mlp_begin(x1, n_chunks):
        hm = _rms(x1, norm_mlp_ref[...]).astype(jnp.bfloat16)
        x2 = x1
        for c in range(n_chunks):
            x2 = mlp_chunk(hm, x2, c)
        return hm, x2, n_chunks

    def mlp_finish(hm, x2, n_done):
        for c in range(n_done, n_ff):
            x2 = mlp_chunk(hm, x2, c)
        if final_norm:
            x2 = _rms(x2, norm_final_ref[...])
        y_ref[...] = x2.reshape(y_ref.shape)

    def store_state():
        for s in range(S):
            for c in range(d_rnn // LANES):
                conv_out_ref[s, :, c * LANES:(c + 1) * LANES] = cbuf[s, c, cpad - c_hist:cpad, :]
            for c in range(d_conv // LANES):
                dw_out_ref[s, :, c * LANES:(c + 1) * LANES] = vbuf[s, c, vpad - v_hist:vpad, :]
            h_out_ref[s] = hcar[s, 0:1, :]

    if pipelined:
        x1_buf = scratch[10]

        @pl.when(n == 0)
        def _no_previous_tile():
            x1_buf[...] = jnp.zeros(x1_buf.shape, x1_buf.dtype)

        z = in_proj()
        mlp_state = mlp_begin(x1_buf[...], n_ff - 1)
        gates = gate_proj(z)
        mlp_finish(*mlp_state)
        x1_buf[...] = mixer_rest(z, *gates)
        pl.when(live)(store_state)
    else:
        z = in_proj()
        x1 = mixer_rest(z, *gate_proj(z))
        if emit_y:
            mlp_finish(*mlp_begin(x1, 0))
        store_state()


def _const_spec(arr):
    nd = arr.ndim
    return pl.BlockSpec(arr.shape, lambda n, _nd=nd: (0,) * _nd, pipeline_mode=pl.Buffered(1))


def _run_layer(x, conv0, h0, dw0, params, *, n_streams, tm, emit_y, final_norm):
    B, T, D = x.shape
    S, TM = n_streams, tm
    assert B % S == 0 and T % TM == 0 and TM % SUBLANES == 0
    rnn_conv_w, d_rnn = params["cw"].shape
    dw_conv_w, d_conv = params["dww"].shape
    assert d_rnn % MXU_DIM == 0 and d_conv % LANES == 0
    cpad = -(-(rnn_conv_w - 1) // SUBLANES) * SUBLANES
    vpad = -(-(dw_conv_w - 1) // SUBLANES) * SUBLANES
    conv_rows = min(64, TM)
    assert TM % conv_rows == 0 and conv_rows % (2 * SUBLANES) == 0
    d_ff = params["w_up"].shape[-1]
    ff_chunk = min(1024, d_ff)
    tps = T // TM
    n_tiles = (B // S) * tps
    pipelined = emit_y and n_tiles > 1

    def in_tile(n):
        return jnp.minimum(n, n_tiles - 1) if pipelined else n

    def out_tile(n):
        return jnp.maximum(n - 1, 0) if pipelined else n

    def state_spec(arr):
        blk = (S,) + arr.shape[1:]
        if arr.shape[0] == B:
            return pl.BlockSpec(blk, lambda n: (in_tile(n) // tps, 0, 0))
        assert arr.shape[0] == 1 and S == 1
        return pl.BlockSpec(blk, lambda n: (0, 0, 0))

    order = ["norm_mix", "w_in", "cw", "cb", "w_gate", "b_r", "b_i", "lam", "dww", "dwb", "ln_g", "ln_b",
             "onr", "onc", "w_out", "norm_mlp", "w_up", "w_down", "norm_final"]
    weights = [params[k] for k in order]
    in_specs = [pl.BlockSpec((S, TM, D), lambda n: (in_tile(n) // tps, in_tile(n) % tps, 0)),
                state_spec(conv0), state_spec(h0), state_spec(dw0)]
    mlp_only = ("w_out", "norm_mlp", "w_up", "w_down", "norm_final")
    in_specs += [_const_spec(w) if emit_y or k not in mlp_only else pl.BlockSpec(memory_space=pl.ANY)
                 for k, w in zip(order, weights)]

    f32 = jnp.float32
    out_shape = [jax.ShapeDtypeStruct((B, rnn_conv_w - 1, d_rnn), f32),
                 jax.ShapeDtypeStruct((B, 1, d_rnn), f32),
                 jax.ShapeDtypeStruct((B, dw_conv_w - 1, d_conv), f32)]
    out_specs = [pl.BlockSpec((S, rnn_conv_w - 1, d_rnn), lambda n: (in_tile(n) // tps, 0, 0)),
                 pl.BlockSpec((S, 1, d_rnn), lambda n: (in_tile(n) // tps, 0, 0)),
                 pl.BlockSpec((S, dw_conv_w - 1, d_conv), lambda n: (in_tile(n) // tps, 0, 0))]
    if emit_y:
        out_shape = [jax.ShapeDtypeStruct((B, T, D), x.dtype)] + out_shape
        out_specs = [pl.BlockSpec((S, TM, D), lambda n: (out_tile(n) // tps, out_tile(n) % tps, 0))] + out_specs

    M = S * TM
    lvl_rows = _scan_level_rows(TM)
    scratch_shapes = [
        pltpu.VMEM((S, d_rnn // LANES, cpad + TM, LANES), f32),
        pltpu.VMEM((S, d_conv // LANES, vpad + TM, LANES), f32),
        pltpu.VMEM((S, SUBLANES, d_rnn), f32),
        pltpu.VMEM((d_rnn // LANES, M, LANES), f32),
        pltpu.VMEM((d_rnn // LANES, M, LANES), f32),
        pltpu.VMEM((d_rnn // LANES, M, LANES), f32),
        pltpu.VMEM((2, S, d_rnn // LANES, max(sum(lvl_rows), SUBLANES), LANES), f32),
        pltpu.VMEM((S, d_rnn // LANES, max(sum(r + 2 * SUBLANES for r in lvl_rows), SUBLANES), LANES), f32),
        pltpu.VMEM((d_rnn // LANES, M, LANES), f32),
        pltpu.VMEM((d_conv // LANES, M, LANES), f32),
    ]
    if pipelined:
        scratch_shapes.append(pltpu.VMEM((M, D), f32))
    kern = functools.partial(_layer_kernel, n_streams=S, tm=TM, n_tiles=n_tiles, tiles_per_stream=tps,
                             emit_y=emit_y, final_norm=final_norm, pipelined=pipelined,
                             rnn_conv_w=rnn_conv_w, dw_conv_w=dw_conv_w, ff_chunk=ff_chunk, conv_rows=conv_rows)
    return pl.pallas_call(
        kern,
        grid=(n_tiles + (1 if pipelined else 0),),
        in_specs=in_specs,
        out_specs=out_specs,
        out_shape=out_shape,
        scratch_shapes=scratch_shapes,
        compiler_params=pltpu.CompilerParams(
            dimension_semantics=("arbitrary",),
            vmem_limit_bytes=VMEM_LIMIT_BYTES),
        name=f"hybrid_layer_s{S}_t{TM}",
    )(x, conv0, h0, dw0, *weights)


def _block_diag_gates(w_r, w_i):
    n_heads, hd, _ = w_r.shape
    per = MXU_DIM // hd
    n_slab = n_heads // per
    on_diag = jnp.eye(per, dtype=bool)[None, :, None, :, None]

    def slabs(w):
        w = w.reshape(n_slab, per, hd, hd)
        bd = jnp.where(on_diag, w[:, :, :, None, :], 0.0)
        return bd.reshape(n_slab, MXU_DIM, MXU_DIM)

    return jnp.concatenate([slabs(w_r), slabs(w_i)], axis=-1).astype(jnp.bfloat16)


def kernel(x_prompt, x_sample, state_rglru_conv, state_rglru_h, state_dwconv, meta_tokens, norm_mix, w_in,
           rnn_conv_w, rnn_conv_b, w_gate_r, b_gate_r, w_gate_i, b_gate_i, rglru_lambda, dw_w, dw_b, ln_conv_g,
           ln_conv_b, out_norm_rnn, out_norm_conv, w_out, norm_mlp, w_up, w_down, norm_final):
    depth = norm_mix.shape[0]
    bf16 = jnp.bfloat16
    d_rnn = rnn_conv_w.shape[-1]
    d_conv = dw_w.shape[-1]
    c_hist = rnn_conv_w.shape[1] - 1
    v_hist = dw_w.shape[1] - 1

    xm = meta_tokens[None].astype(x_prompt.dtype)
    xp, xs = x_prompt, x_sample
    outs_p, outs_s = [], []
    for l in range(depth):
        last = l == depth - 1
        params = dict(
            norm_mix=norm_mix[l][None], w_in=w_in[l].astype(bf16), cw=rnn_conv_w[l], cb=rnn_conv_b[l][None],
            w_gate=_block_diag_gates(w_gate_r[l], w_gate_i[l]), b_r=b_gate_r[l][None], b_i=b_gate_i[l][None],
            lam=rglru_lambda[l][None], dww=dw_w[l], dwb=dw_b[l][None], ln_g=ln_conv_g[l][None],
            ln_b=ln_conv_b[l][None], onr=out_norm_rnn[l][None], onc=out_norm_conv[l][None],
            w_out=w_out[l].astype(bf16), norm_mlp=norm_mlp[l][None], w_up=w_up[l].astype(bf16),
            w_down=w_down[l].astype(bf16), norm_final=norm_final[None])
        f32 = jnp.float32
        zero_state = (jnp.zeros((1, c_hist, d_rnn), f32), jnp.zeros((1, 1, d_rnn), f32),
                      jnp.zeros((1, v_hist, d_conv), f32))
        res_m = _run_layer(xm, *zero_state, params, n_streams=1, tm=xm.shape[1], emit_y=not last,
                           final_norm=False)
        if not last:
            xm, res_m = res_m[0], res_m[1:]
        res_p = _run_layer(xp, *res_m, params, n_streams=1, tm=min(512, xp.shape[1]), emit_y=True,
                           final_norm=last)
        res_s = _run_layer(xs, state_rglru_conv[l], state_rglru_h[l][:, None], state_dwconv[l], params,
                           n_streams=xs.shape[0], tm=xs.shape[1], emit_y=True, final_norm=last)
        xp, xs = res_p[0], res_s[0]
        outs_p.append(res_p[1:])
        outs_s.append(res_s[1:])

    def stack(outs, i):
        return outs[0][i][None] if depth == 1 else jnp.stack([o[i] for o in outs])

    return (xp, xs,
            stack(outs_p, 0), stack(outs_p, 1)[:, :, 0], stack(outs_p, 2),
            stack(outs_s, 0), stack(outs_s, 1)[:, :, 0], stack(outs_s, 2))
```

```python
import functools
import math

import jax
import jax.numpy as jnp
from jax import lax
from jax.experimental import pallas as pl
from jax.experimental.pallas import tpu as pltpu

EPS = 1e-6
RGLRU_C = 8.0
SUBLANES = 8
LANES = 128
MXU_DIM = 256
VMEM_LIMIT_BYTES = 60000 * 1024


def _rms(x, g):
    ms = jnp.mean(x * x, axis=-1, keepdims=True)
    return x * lax.rsqrt(ms + EPS) * g


def _sigmoid(x):
    return 1.0 / (1.0 + jnp.exp(-x))


def _gelu_tanh(x):
    cdf = 0.5 * (1.0 + jnp.tanh(math.sqrt(2.0 / math.pi) * (x + 0.044715 * (x * x * x))))
    return x * cdf


def _bf16_dot(a, b):
    return jnp.dot(a.astype(jnp.bfloat16), b, preferred_element_type=jnp.float32)


def _to_slabs(dst, idx, x):
    for c in range(x.shape[-1] // LANES):
        dst[idx + (c,)] = x[:, c * LANES:(c + 1) * LANES]


def _from_slabs(buf):
    return jnp.concatenate([buf[c] for c in range(buf.shape[0])], axis=-1)


def _strided_causal_conv(win, s, w_ref, b_ref, out, *, taps, lead, tm, rows):
    half = rows // 2
    for c in range(win.shape[1]):
        lanes = slice(c * LANES, (c + 1) * LANES)
        for t0 in [i * rows + par for i in range(tm // rows) for par in range(2)]:
            acc = win[s, c, pl.ds(lead + t0, half, stride=2), :] * w_ref[0:1, lanes]
            for k in range(1, taps):
                acc = acc + win[s, c, pl.ds(lead + t0 + k, half, stride=2), :] * w_ref[k:k + 1, lanes]
            out[c, pl.ds(s * tm + t0, half, stride=2), :] = acc + b_ref[:, lanes]


def _scan_radices(n_rows):
    n_groups, radices = n_rows // SUBLANES, []
    while n_groups > 1:
        assert n_groups % 2 == 0
        r = 4 if n_groups % 4 == 0 else 2
        radices.append(r)
        n_groups //= r
    return radices


def _scan_level_rows(n_rows):
    rows, out = n_rows, []
    for r in _scan_radices(n_rows)[:-1]:
        rows //= r
        out.append(rows)
    return out


def _linear_scan_slab(a_ref, b_ref, h_ref, ab_lvl, hin_lvl, base, n_rows, carry):
    radices = _scan_radices(n_rows)
    lvl_rows = _scan_level_rows(n_rows)
    ab_off = [sum(lvl_rows[:i]) for i in range(len(lvl_rows))]
    hin_off = [sum(r + 2 * SUBLANES for r in lvl_rows[:i]) for i in range(len(lvl_rows))]

    partial = []
    rows = n_rows
    for lvl, r in enumerate(radices):
        g = rows // r
        src, off = ((a_ref, b_ref), base) if lvl == 0 else ((ab_lvl.at[0], ab_lvl.at[1]), ab_off[lvl - 1])
        ld = lambda kind, k: src[kind][pl.ds(off + k, g, stride=r), :]
        pa, pb = ld(0, 0), ld(1, 0)
        maps = [(pa, pb)]
        for k in range(1, r):
            ak, bk = ld(0, k), ld(1, k)
            pa, pb = ak * pa, ak * pb + bk
            maps.append((pa, pb))
        partial.append(maps)
        rows = g
        if lvl + 1 < len(radices):
            ab_lvl[0, ab_off[lvl]:ab_off[lvl] + g, :] = pa
            ab_lvl[1, ab_off[lvl]:ab_off[lvl] + g, :] = pb
    if not radices:
        pa, pb = a_ref[base:base + SUBLANES, :], b_ref[base:base + SUBLANES, :]

    row = lax.broadcasted_iota(jnp.int32, (SUBLANES, LANES), 0)
    for sh in (1, 2, 4):
        keep = row >= sh
        pa_s = jnp.where(keep, pltpu.roll(pa, sh, 0), 1.0)
        pb_s = jnp.where(keep, pltpu.roll(pb, sh, 0), 0.0)
        pa, pb = pa * pa_s, pa * pb_s + pb
    h_top = pa * carry + pb
    new_carry = jnp.broadcast_to(h_top[SUBLANES - 1:SUBLANES, :], (SUBLANES, LANES))
    if not radices:
        h_ref[base:base + SUBLANES, :] = h_top
        return new_carry
    h_prev = jnp.where(row == 0, carry, pltpu.roll(h_top, 1, 0))

    for lvl in reversed(range(len(radices))):
        r = radices[lvl]
        g = partial[lvl][0][0].shape[0]
        if lvl < len(radices) - 1:
            o = hin_off[lvl] + SUBLANES
            h_prev = hin_lvl[o:o + g, :]
        for k, (pa, pb) in enumerate(partial[lvl]):
            hk = pa * h_prev + pb
            if lvl == 0:
                h_ref[pl.ds(base + k, g, stride=r), :] = hk
            else:
                o = hin_off[lvl - 1] + SUBLANES
                hin_lvl[pl.ds(o + 1 + k, g, stride=r), :] = hk
        if lvl > 0:
            o = hin_off[lvl - 1] + SUBLANES
            hin_lvl[o:o + 1, :] = carry[0:1, :]
    return new_carry


def _layer_kernel(x_ref, conv0_ref, h0_ref, dw0_ref,
                  norm_mix_ref, w_in_ref, cw_ref, cb_ref, w_gate_ref, b_r_ref, b_i_ref, lam_ref,
                  dww_ref, dwb_ref, ln_g_ref, ln_b_ref, onr_ref, onc_ref,
                  w_out_ref, norm_mlp_ref, w_up_ref, w_down_ref, norm_final_ref,
                  *rest, n_streams, tm, n_tiles, tiles_per_stream, emit_y, final_norm, pipelined,
                  rnn_conv_w, dw_conv_w, ff_chunk, conv_rows):
    if emit_y:
        y_ref, conv_out_ref, h_out_ref, dw_out_ref = rest[:4]
        scratch = rest[4:]
    else:
        conv_out_ref, h_out_ref, dw_out_ref = rest[:3]
        scratch = rest[3:]
    cbuf, vbuf, hcar, a_buf, b_buf, h_buf, ab_lvl, hin_lvl, xc_buf, vc_buf = scratch[:10]

    S, TM = n_streams, tm
    M = S * TM
    d_rnn = cw_ref.shape[-1]
    d_conv = dww_ref.shape[-1]
    cpad = cbuf.shape[2] - TM
    vpad = vbuf.shape[2] - TM
    c_hist = rnn_conv_w - 1
    v_hist = dw_conv_w - 1
    n = pl.program_id(0)
    tile = jnp.minimum(n, n_tiles - 1)
    live = n < n_tiles

    @pl.when(jnp.logical_and(lax.rem(tile, tiles_per_stream) == 0, live))
    def _load_state():
        cbuf[:, :, 0:cpad - c_hist, :] = jnp.zeros((S, d_rnn // LANES, cpad - c_hist, LANES), jnp.float32)
        vbuf[:, :, 0:vpad - v_hist, :] = jnp.zeros((S, d_conv // LANES, vpad - v_hist, LANES), jnp.float32)
        for c in range(d_rnn // LANES):
            cbuf[:, c, cpad - c_hist:cpad, :] = conv0_ref[:, :, c * LANES:(c + 1) * LANES]
        for c in range(d_conv // LANES):
            vbuf[:, c, vpad - v_hist:vpad, :] = dw0_ref[:, :, c * LANES:(c + 1) * LANES]
        hcar[...] = jnp.broadcast_to(h0_ref[...], hcar.shape)

    def in_proj():
        x = x_ref[...].reshape(M, x_ref.shape[-1])
        hn = _rms(x, norm_mix_ref[...])
        return _bf16_dot(hn, w_in_ref[...])

    def gate_proj(z):
        xr = z[:, 0:d_rnn]
        for s in range(S):
            _to_slabs(cbuf.at[:, :, cpad:cpad + TM, :], (s,), xr[s * TM:(s + 1) * TM, :])
            _strided_causal_conv(cbuf, s, cw_ref, cb_ref, xc_buf, taps=rnn_conv_w, lead=cpad - c_hist, tm=TM,
                                 rows=conv_rows)
            for c in range(d_rnn // LANES):
                cbuf[s, c, 0:cpad, :] = cbuf[s, c, TM:TM + cpad, :]
        xr_c = _from_slabs(xc_buf)

        xb = xr_c.astype(jnp.bfloat16)
        r_parts, i_parts = [], []
        for j in range(d_rnn // MXU_DIM):
            ri = jnp.dot(xb[:, j * MXU_DIM:(j + 1) * MXU_DIM], w_gate_ref[j], preferred_element_type=jnp.float32)
            r_parts.append(ri[:, :MXU_DIM])
            i_parts.append(ri[:, MXU_DIM:])
        return xr_c, r_parts, i_parts

    def mixer_rest(z, xr_c, r_parts, i_parts):
        x = x_ref[...].reshape(M, x_ref.shape[-1])
        gate = z[:, d_rnn:2 * d_rnn]
        glu_v = z[:, 2 * d_rnn:2 * d_rnn + d_conv]
        glu_g = z[:, 2 * d_rnn + d_conv:]
        r = _sigmoid(jnp.concatenate(r_parts, axis=-1) + b_r_ref[...])
        ig = _sigmoid(jnp.concatenate(i_parts, axis=-1) + b_i_ref[...])
        nl = -lam_ref[...]
        softplus_nl = jnp.maximum(nl, 0.0) + jnp.log1p(jnp.exp(-jnp.abs(nl)))
        log_a = (-RGLRU_C * r) * softplus_nl
        a = jnp.exp(log_a)
        mult = jnp.sqrt(jnp.maximum(1.0 - a * a, 0.0))
        _to_slabs(a_buf, (), a)
        _to_slabs(b_buf, (), mult * ig * xr_c)
        for s in range(S):
            carry = hcar[s]
            hcar[s] = jnp.concatenate(
                [_linear_scan_slab(a_buf.at[c], b_buf.at[c], h_buf.at[c], ab_lvl.at[:, s, c], hin_lvl.at[s, c],
                                   s * TM, TM, carry[:, c * LANES:(c + 1) * LANES])
                 for c in range(d_rnn // LANES)], axis=-1)

        v = glu_v * _sigmoid(glu_g)
        for s in range(S):
            _to_slabs(vbuf.at[:, :, vpad:vpad + TM, :], (s,), v[s * TM:(s + 1) * TM, :])
            _strided_causal_conv(vbuf, s, dww_ref, dwb_ref, vc_buf, taps=dw_conv_w, lead=vpad - v_hist, tm=TM,
                                 rows=conv_rows)
            for c in range(d_conv // LANES):
                vbuf[s, c, 0:vpad, :] = vbuf[s, c, TM:TM + vpad, :]
        if not emit_y:
            return None

        y_rnn = _from_slabs(h_buf) * _gelu_tanh(gate)
        vc = _from_slabs(vc_buf)
        mu = jnp.mean(vc, axis=-1, keepdims=True)
        vcc = vc - mu
        ln = vcc * lax.rsqrt(jnp.mean(vcc * vcc, axis=-1, keepdims=True) + EPS) * ln_g_ref[...] + ln_b_ref[...]
        y_conv = ln * _sigmoid(ln)
        mix = jnp.concatenate([_rms(y_rnn, onr_ref[...]), _rms(y_conv, onc_ref[...])], axis=-1)
        return x + _bf16_dot(mix, w_out_ref[...])

    n_ff = w_up_ref.shape[-1] // ff_chunk

    def mlp_chunk(hm, x2, c):
        hc = jnp.dot(hm, w_up_ref[:, c * ff_chunk:(c + 1) * ff_chunk], preferred_element_type=jnp.float32)
        hc = jnp.square(jnp.maximum(hc, 0.0))
        return x2 + _bf16_dot(hc, w_down_ref[c * ff_chunk:(c + 1) * ff_chunk, :])

    def mlp_begin(x1, n_chunks):
        hm = _rms(x1, norm_mlp_ref[...]).astype(jnp.bfloat16)
        x2 = x1
        for c in range(n_chunks):
            x2 = mlp_chunk(hm, x2, c)
        return hm, x2, n_chunks

    def mlp_finish(hm, x2, n_done):
        for c in range(n_done, n_ff):
            x2 = mlp_chunk(hm, x2, c)
        if final_norm:
            x2 = _rms(x2, norm_final_ref[...])
        y_ref[...] = x2.reshape(y_ref.shape)

    def store_state():
        for s in range(S):
            for c in range(d_rnn // LANES):
                conv_out_ref[s, :, c * LANES:(c + 1) * LANES] = cbuf[s, c, cpad - c_hist:cpad, :]
            for c in range(d_conv // LANES):
                dw_out_ref[s, :, c * LANES:(c + 1) * LANES] = vbuf[s, c, vpad - v_hist:vpad, :]
            h_out_ref[s] = hcar[s, 0:1, :]

    if pipelined:
        x1_buf = scratch[10]

        @pl.when(n == 0)
        def _first_step():
            z = in_proj()
            x1_buf[...] = mixer_rest(z, *gate_proj(z))

        @pl.when(jnp.logical_and(n > 0, live))
        def _steady_step():
            z = in_proj()
            mlp_state = mlp_begin(x1_buf[...], n_ff - 1)
            gates = gate_proj(z)
            mlp_finish(*mlp_state)
            x1_buf[...] = mixer_rest(z, *gates)

        @pl.when(n == n_tiles)
        def _last_step():
            mlp_finish(*mlp_begin(x1_buf[...], 0))

        pl.when(live)(store_state)
    else:
        z = in_proj()
        x1 = mixer_rest(z, *gate_proj(z))
        if emit_y:
            mlp_finish(*mlp_begin(x1, 0))
        store_state()


def _const_spec(arr):
    nd = arr.ndim
    return pl.BlockSpec(arr.shape, lambda n, _nd=nd: (0,) * _nd, pipeline_mode=pl.Buffered(1))


def _run_layer(x, conv0, h0, dw0, params, *, n_streams, tm, emit_y, final_norm):
    B, T, D = x.shape
    S, TM = n_streams, tm
    assert B % S == 0 and T % TM == 0 and TM % SUBLANES == 0
    rnn_conv_w, d_rnn = params["cw"].shape
    dw_conv_w, d_conv = params["dww"].shape
    assert d_rnn % MXU_DIM == 0 and d_conv % LANES == 0
    cpad = -(-(rnn_conv_w - 1) // SUBLANES) * SUBLANES
    vpad = -(-(dw_conv_w - 1) // SUBLANES) * SUBLANES
    conv_rows = min(64, TM)
    assert TM % conv_rows == 0 and conv_rows % (2 * SUBLANES) == 0
    d_ff = params["w_up"].shape[-1]
    ff_chunk = min(1024, d_ff)
    tps = T // TM
    n_tiles = (B // S) * tps
    pipelined = emit_y and n_tiles > 1

    def in_tile(n):
        return jnp.minimum(n, n_tiles - 1) if pipelined else n

    def out_tile(n):
        return jnp.maximum(n - 1, 0) if pipelined else n

    def state_spec(arr):
        blk = (S,) + arr.shape[1:]
        if arr.shape[0] == B:
            return pl.BlockSpec(blk, lambda n: (in_tile(n) // tps, 0, 0))
        assert arr.shape[0] == 1 and S == 1
        return pl.BlockSpec(blk, lambda n: (0, 0, 0))

    order = ["norm_mix", "w_in", "cw", "cb", "w_gate", "b_r", "b_i", "lam", "dww", "dwb", "ln_g", "ln_b",
             "onr", "onc", "w_out", "norm_mlp", "w_up", "w_down", "norm_final"]
    weights = [params[k] for k in order]
    in_specs = [pl.BlockSpec((S, TM, D), lambda n: (in_tile(n) // tps, in_tile(n) % tps, 0)),
                state_spec(conv0), state_spec(h0), state_spec(dw0)]
    mlp_only = ("w_out", "norm_mlp", "w_up", "w_down", "norm_final")
    in_specs += [_const_spec(w) if emit_y or k not in mlp_only else pl.BlockSpec(memory_space=pl.ANY)
                 for k, w in zip(order, weights)]

    f32 = jnp.float32
    out_shape = [jax.ShapeDtypeStruct((B, rnn_conv_w - 1, d_rnn), f32),
                 jax.ShapeDtypeStruct((B, 1, d_rnn), f32),
                 jax.ShapeDtypeStruct((B, dw_conv_w - 1, d_conv), f32)]
    out_specs = [pl.BlockSpec((S, rnn_conv_w - 1, d_rnn), lambda n: (in_tile(n) // tps, 0, 0)),
                 pl.BlockSpec((S, 1, d_rnn), lambda n: (in_tile(n) // tps, 0, 0)),
                 pl.BlockSpec((S, dw_conv_w - 1, d_conv), lambda n: (in_tile(n) // tps, 0, 0))]
    if emit_y:
        out_shape = [jax.ShapeDtypeStruct((B, T, D), x.dtype)] + out_shape
        out_specs = [pl.BlockSpec((S, TM, D), lambda n: (out_tile(n) // tps, out_tile(n) % tps, 0))] + out_specs

    M = S * TM
    lvl_rows = _scan_level_rows(TM)
    scratch_shapes = [
        pltpu.VMEM((S, d_rnn // LANES, cpad + TM, LANES), f32),
        pltpu.VMEM((S, d_conv // LANES, vpad + TM, LANES), f32),
        pltpu.VMEM((S, SUBLANES, d_rnn), f32),
        pltpu.VMEM((d_rnn // LANES, M, LANES), f32),
        pltpu.VMEM((d_rnn // LANES, M, LANES), f32),
        pltpu.VMEM((d_rnn // LANES, M, LANES), f32),
        pltpu.VMEM((2, S, d_rnn // LANES, max(sum(lvl_rows), SUBLANES), LANES), f32),
        pltpu.VMEM((S, d_rnn // LANES, max(sum(r + 2 * SUBLANES for r in lvl_rows), SUBLANES), LANES), f32),
        pltpu.VMEM((d_rnn // LANES, M, LANES), f32),
        pltpu.VMEM((d_conv // LANES, M, LANES), f32),
    ]
    if pipelined:
        scratch_shapes.append(pltpu.VMEM((M, D), f32))
    kern = functools.partial(_layer_kernel, n_streams=S, tm=TM, n_tiles=n_tiles, tiles_per_stream=tps,
                             emit_y=emit_y, final_norm=final_norm, pipelined=pipelined,
                             rnn_conv_w=rnn_conv_w, dw_conv_w=dw_conv_w, ff_chunk=ff_chunk, conv_rows=conv_rows)
    return pl.pallas_call(
        kern,
        grid=(n_tiles + (1 if pipelined else 0),),
        in_specs=in_specs,
        out_specs=out_specs,
        out_shape=out_shape,
        scratch_shapes=scratch_shapes,
        compiler_params=pltpu.CompilerParams(
            dimension_semantics=("arbitrary",),
            vmem_limit_bytes=VMEM_LIMIT_BYTES),
        name=f"hybrid_layer_s{S}_t{TM}",
    )(x, conv0, h0, dw0, *weights)


def _block_diag_gates(w_r, w_i):
    n_heads, hd, _ = w_r.shape
    per = MXU_DIM // hd
    n_slab = n_heads // per
    on_diag = jnp.eye(per, dtype=bool)[None, :, None, :, None]

    def slabs(w):
        w = w.reshape(n_slab, per, hd, hd)
        bd = jnp.where(on_diag, w[:, :, :, None, :], 0.0)
        return bd.reshape(n_slab, MXU_DIM, MXU_DIM)

    return jnp.concatenate([slabs(w_r), slabs(w_i)], axis=-1).astype(jnp.bfloat16)


def kernel(x_prompt, x_sample, state_rglru_conv, state_rglru_h, state_dwconv, meta_tokens, norm_mix, w_in,
           rnn_conv_w, rnn_conv_b, w_gate_r, b_gate_r, w_gate_i, b_gate_i, rglru_lambda, dw_w, dw_b, ln_conv_g,
           ln_conv_b, out_norm_rnn, out_norm_conv, w_out, norm_mlp, w_up, w_down, norm_final):
    depth = norm_mix.shape[0]
    bf16 = jnp.bfloat16
    d_rnn = rnn_conv_w.shape[-1]
    d_conv = dw_w.shape[-1]
    c_hist = rnn_conv_w.shape[1] - 1
    v_hist = dw_w.shape[1] - 1

    xm = meta_tokens[None].astype(x_prompt.dtype)
    xp, xs = x_prompt, x_sample
    outs_p, outs_s = [], []
    for l in range(depth):
        last = l == depth - 1
        params = dict(
            norm_mix=norm_mix[l][None], w_in=w_in[l].astype(bf16), cw=rnn_conv_w[l], cb=rnn_conv_b[l][None],
            w_gate=_block_diag_gates(w_gate_r[l], w_gate_i[l]), b_r=b_gate_r[l][None], b_i=b_gate_i[l][None],
            lam=rglru_lambda[l][None], dww=dw_w[l], dwb=dw_b[l][None], ln_g=ln_conv_g[l][None],
            ln_b=ln_conv_b[l][None], onr=out_norm_rnn[l][None], onc=out_norm_conv[l][None],
            w_out=w_out[l].astype(bf16), norm_mlp=norm_mlp[l][None], w_up=w_up[l].astype(bf16),
            w_down=w_down[l].astype(bf16), norm_final=norm_final[None])
        f32 = jnp.float32
        zero_state = (jnp.zeros((1, c_hist, d_rnn), f32), jnp.zeros((1, 1, d_rnn), f32),
                      jnp.zeros((1, v_hist, d_conv), f32))
        res_m = _run_layer(xm, *zero_state, params, n_streams=1, tm=xm.shape[1], emit_y=not last,
                           final_norm=False)
        if not last:
            xm, res_m = res_m[0], res_m[1:]
        res_p = _run_layer(xp, *res_m, params, n_streams=1, tm=min(512, xp.shape[1]), emit_y=True,
                           final_norm=last)
        res_s = _run_layer(xs, state_rglru_conv[l], state_rglru_h[l][:, None], state_dwconv[l], params,
                           n_streams=xs.shape[0], tm=xs.shape[1], emit_y=True, final_norm=last)
        xp, xs = res_p[0], res_s[0]
        outs_p.append(res_p[1:])
        outs_s.append(res_s[1:])

    def stack(outs, i):
        return outs[0][i][None] if depth == 1 else jnp.stack([o[i] for o in outs])

    return (xp, xs,
            stack(outs_p, 0), stack(outs_p, 1)[:, :, 0], stack(outs_p, 2),
            stack(outs_s, 0), stack(outs_s, 1)[:, :, 0], stack(outs_s, 2))
```

```python
import functools
import math

import jax
import jax.numpy as jnp
from jax import lax
from jax.experimental import pallas as pl
from jax.experimental.pallas import tpu as pltpu

EPS = 1e-6
RGLRU_C = 8.0
SUBLANES = 8
LANES = 128
MXU_DIM = 256
VMEM_LIMIT_BYTES = 60000 * 1024


def _rms(x, g):
    ms = jnp.mean(x * x, axis=-1, keepdims=True)
    return x * lax.rsqrt(ms + EPS) * g


def _sigmoid(x):
    return 0.5 * jnp.tanh(0.5 * x) + 0.5


def _silu(x):
    h = 0.5 * x
    return h * (jnp.tanh(h) + 1.0)


def _gelu_tanh(x):
    c = math.sqrt(2.0 / math.pi)
    inner = x * (c + (0.044715 * c) * (x * x))
    return (0.5 * x) * (1.0 + jnp.tanh(inner))


def _bf16_dot(a, b):
    return jnp.dot(a.astype(jnp.bfloat16), b, preferred_element_type=jnp.float32)


def _to_slabs(dst, idx, x):
    for c in range(x.shape[-1] // LANES):
        dst[idx + (c,)] = x[:, c * LANES:(c + 1) * LANES]


def _from_slabs(buf):
    return jnp.concatenate([buf[c] for c in range(buf.shape[0])], axis=-1)


def _strided_causal_conv(win, s, w_ref, b_ref, out, *, taps, lead, tm, rows):
    half = rows // 2
    for c in range(win.shape[1]):
        lanes = slice(c * LANES, (c + 1) * LANES)
        for t0 in [i * rows + par for i in range(tm // rows) for par in range(2)]:
            acc = win[s, c, pl.ds(lead + t0, half, stride=2), :] * w_ref[0:1, lanes]
            for k in range(1, taps):
                acc = acc + win[s, c, pl.ds(lead + t0 + k, half, stride=2), :] * w_ref[k:k + 1, lanes]
            out[c, pl.ds(s * tm + t0, half, stride=2), :] = acc + b_ref[:, lanes]


def _scan_radices(n_rows):
    n_groups, radices = n_rows // SUBLANES, []
    while n_groups > 1:
        assert n_groups % 2 == 0
        r = 4 if n_groups % 4 == 0 else 2
        radices.append(r)
        n_groups //= r
    return radices


def _scan_level_rows(n_rows):
    rows, out = n_rows, []
    for r in _scan_radices(n_rows)[:-1]:
        rows //= r
        out.append(rows)
    return out


def _linear_scan_slab(a_ref, b_ref, h_ref, ab_lvl, hin_lvl, base, n_rows, carry):
    radices = _scan_radices(n_rows)
    lvl_rows = _scan_level_rows(n_rows)
    ab_off = [sum(lvl_rows[:i]) for i in range(len(lvl_rows))]
    hin_off = [sum(r + 2 * SUBLANES for r in lvl_rows[:i]) for i in range(len(lvl_rows))]

    partial = []
    rows = n_rows
    for lvl, r in enumerate(radices):
        g = rows // r
        src, off = ((a_ref, b_ref), base) if lvl == 0 else ((ab_lvl.at[0], ab_lvl.at[1]), ab_off[lvl - 1])
        ld = lambda kind, k: src[kind][pl.ds(off + k, g, stride=r), :]
        pa, pb = ld(0, 0), ld(1, 0)
        maps = [(pa, pb)]
        for k in range(1, r):
            ak, bk = ld(0, k), ld(1, k)
            pa, pb = ak * pa, ak * pb + bk
            maps.append((pa, pb))
        partial.append(maps)
        rows = g
        if lvl + 1 < len(radices):
            ab_lvl[0, ab_off[lvl]:ab_off[lvl] + g, :] = pa
            ab_lvl[1, ab_off[lvl]:ab_off[lvl] + g, :] = pb
    if not radices:
        pa, pb = a_ref[base:base + SUBLANES, :], b_ref[base:base + SUBLANES, :]

    row = lax.broadcasted_iota(jnp.int32, (SUBLANES, LANES), 0)
    for sh in (1, 2, 4):
        keep = row >= sh
        pa_s = jnp.where(keep, pltpu.roll(pa, sh, 0), 1.0)
        pb_s = jnp.where(keep, pltpu.roll(pb, sh, 0), 0.0)
        pa, pb = pa * pa_s, pa * pb_s + pb
    h_top = pa * carry + pb
    new_carry = jnp.broadcast_to(h_top[SUBLANES - 1:SUBLANES, :], (SUBLANES, LANES))
    if not radices:
        h_ref[base:base + SUBLANES, :] = h_top
        return new_carry
    h_prev = jnp.where(row == 0, carry, pltpu.roll(h_top, 1, 0))

    for lvl in reversed(range(len(radices))):
        r = radices[lvl]
        g = partial[lvl][0][0].shape[0]
        if lvl < len(radices) - 1:
            o = hin_off[lvl] + SUBLANES
            h_prev = hin_lvl[o:o + g, :]
        for k, (pa, pb) in enumerate(partial[lvl]):
            hk = pa * h_prev + pb
            if lvl == 0:
                h_ref[pl.ds(base + k, g, stride=r), :] = hk
            else:
                o = hin_off[lvl - 1] + SUBLANES
                hin_lvl[pl.ds(o + 1 + k, g, stride=r), :] = hk
        if lvl > 0:
            o = hin_off[lvl - 1] + SUBLANES
            hin_lvl[o:o + 1, :] = carry[0:1, :]
    return new_carry


def _layer_kernel(x_ref, conv0_ref, h0_ref, dw0_ref,
                  norm_mix_ref, w_in_ref, cw_ref, cb_ref, w_gate_ref, b_r_ref, b_i_ref, lam_ref,
                  dww_ref, dwb_ref, ln_g_ref, ln_b_ref, onr_ref, onc_ref,
                  w_out_ref, norm_mlp_ref, w_up_ref, w_down_ref, norm_final_ref,
                  *rest, n_streams, tm, n_tiles, tiles_per_stream, emit_y, final_norm, pipelined,
                  rnn_conv_w, dw_conv_w, ff_chunk, conv_rows):
    if emit_y:
        y_ref, conv_out_ref, h_out_ref, dw_out_ref = rest[:4]
        scratch = rest[4:]
    else:
        conv_out_ref, h_out_ref, dw_out_ref = rest[:3]
        scratch = rest[3:]
    cbuf, vbuf, hcar, a_buf, b_buf, h_buf, ab_lvl, hin_lvl, xc_buf, vc_buf = scratch[:10]

    S, TM = n_streams, tm
    M = S * TM
    d_rnn = cw_ref.shape[-1]
    d_conv = dww_ref.shape[-1]
    cpad = cbuf.shape[2] - TM
    vpad = vbuf.shape[2] - TM
    c_hist = rnn_conv_w - 1
    v_hist = dw_conv_w - 1
    n = pl.program_id(0)
    tile = jnp.minimum(n, n_tiles - 1)
    live = n < n_tiles

    @pl.when(jnp.logical_and(lax.rem(tile, tiles_per_stream) == 0, live))
    def _load_state():
        cbuf[:, :, 0:cpad - c_hist, :] = jnp.zeros((S, d_rnn // LANES, cpad - c_hist, LANES), jnp.float32)
        vbuf[:, :, 0:vpad - v_hist, :] = jnp.zeros((S, d_conv // LANES, vpad - v_hist, LANES), jnp.float32)
        for c in range(d_rnn // LANES):
            cbuf[:, c, cpad - c_hist:cpad, :] = conv0_ref[:, :, c * LANES:(c + 1) * LANES]
        for c in range(d_conv // LANES):
            vbuf[:, c, vpad - v_hist:vpad, :] = dw0_ref[:, :, c * LANES:(c + 1) * LANES]
        hcar[...] = jnp.broadcast_to(h0_ref[...], hcar.shape)

    def in_proj():
        x = x_ref[...].reshape(M, x_ref.shape[-1])
        hn = _rms(x, norm_mix_ref[...])
        return _bf16_dot(hn, w_in_ref[...])

    def gate_proj(z):
        xr = z[:, 0:d_rnn]
        for s in range(S):
            _to_slabs(cbuf.at[:, :, cpad:cpad + TM, :], (s,), xr[s * TM:(s + 1) * TM, :])
            _strided_causal_conv(cbuf, s, cw_ref, cb_ref, xc_buf, taps=rnn_conv_w, lead=cpad - c_hist, tm=TM,
                                 rows=conv_rows)
            for c in range(d_rnn // LANES):
                cbuf[s, c, 0:cpad, :] = cbuf[s, c, TM:TM + cpad, :]
        xr_c = _from_slabs(xc_buf)

        xb = xr_c.astype(jnp.bfloat16)
        r_parts, i_parts = [], []
        for j in range(d_rnn // MXU_DIM):
            ri = jnp.dot(xb[:, j * MXU_DIM:(j + 1) * MXU_DIM], w_gate_ref[j], preferred_element_type=jnp.float32)
            r_parts.append(ri[:, :MXU_DIM])
            i_parts.append(ri[:, MXU_DIM:])
        return xr_c, r_parts, i_parts

    def mixer_rest(z, xr_c, r_parts, i_parts):
        x = x_ref[...].reshape(M, x_ref.shape[-1])
        gate = z[:, d_rnn:2 * d_rnn]
        glu_v = z[:, 2 * d_rnn:2 * d_rnn + d_conv]
        glu_g = z[:, 2 * d_rnn + d_conv:]
        r = _sigmoid(jnp.concatenate(r_parts, axis=-1) + b_r_ref[...])
        ig = _sigmoid(jnp.concatenate(i_parts, axis=-1) + b_i_ref[...])
        nl = -lam_ref[...]
        softplus_nl = jnp.maximum(nl, 0.0) + jnp.log1p(jnp.exp(-jnp.abs(nl)))
        log_a = (-RGLRU_C * r) * softplus_nl
        a = jnp.exp(log_a)
        mult = jnp.sqrt(jnp.maximum(1.0 - a * a, 0.0))
        _to_slabs(a_buf, (), a)
        _to_slabs(b_buf, (), mult * ig * xr_c)
        for s in range(S):
            carry = hcar[s]
            hcar[s] = jnp.concatenate(
                [_linear_scan_slab(a_buf.at[c], b_buf.at[c], h_buf.at[c], ab_lvl.at[:, s, c], hin_lvl.at[s, c],
                                   s * TM, TM, carry[:, c * LANES:(c + 1) * LANES])
                 for c in range(d_rnn // LANES)], axis=-1)

        v = glu_v * _sigmoid(glu_g)
        for s in range(S):
            _to_slabs(vbuf.at[:, :, vpad:vpad + TM, :], (s,), v[s * TM:(s + 1) * TM, :])
            _strided_causal_conv(vbuf, s, dww_ref, dwb_ref, vc_buf, taps=dw_conv_w, lead=vpad - v_hist, tm=TM,
                                 rows=conv_rows)
            for c in range(d_conv // LANES):
                vbuf[s, c, 0:vpad, :] = vbuf[s, c, TM:TM + vpad, :]
        if not emit_y:
            return None

        y_rnn = _from_slabs(h_buf) * _gelu_tanh(gate)
        vc = _from_slabs(vc_buf)
        mu = jnp.mean(vc, axis=-1, keepdims=True)
        vcc = vc - mu
        ln = vcc * lax.rsqrt(jnp.mean(vcc * vcc, axis=-1, keepdims=True) + EPS) * ln_g_ref[...] + ln_b_ref[...]
        y_conv = _silu(ln)
        mix = jnp.concatenate([_rms(y_rnn, onr_ref[...]), _rms(y_conv, onc_ref[...])], axis=-1)
        return x + _bf16_dot(mix, w_out_ref[...])

    n_ff = w_up_ref.shape[-1] // ff_chunk

    def mlp_chunk(hm, x2, c):
        hc = jnp.dot(hm, w_up_ref[:, c * ff_chunk:(c + 1) * ff_chunk], preferred_element_type=jnp.float32)
        hc = jnp.square(jnp.maximum(hc, 0.0))
        return x2 + _bf16_dot(hc, w_down_ref[c * ff_chunk:(c + 1) * ff_chunk, :])

    def mlp_begin(x1, n_chunks):
        hm = _rms(x1, norm_mlp_ref[...]).astype(jnp.bfloat16)
        x2 = x1
        for c in range(n_chunks):
            x2 = mlp_chunk(hm, x2, c)
        return hm, x2, n_chunks

    def mlp_finish(hm, x2, n_done):
        for c in range(n_done, n_ff):
            x2 = mlp_chunk(hm, x2, c)
        if final_norm:
            x2 = _rms(x2, norm_final_ref[...])
        y_ref[...] = x2.reshape(y_ref.shape)

    def store_state():
        for s in range(S):
            for c in range(d_rnn // LANES):
                conv_out_ref[s, :, c * LANES:(c + 1) * LANES] = cbuf[s, c, cpad - c_hist:cpad, :]
            for c in range(d_conv // LANES):
                dw_out_ref[s, :, c * LANES:(c + 1) * LANES] = vbuf[s, c, vpad - v_hist:vpad, :]
            h_out_ref[s] = hcar[s, 0:1, :]

    if pipelined:
        x1_buf = scratch[10]

        @pl.when(n == 0)
        def _first_step():
            z = in_proj()
            x1_buf[...] = mixer_rest(z, *gate_proj(z))

        @pl.when(jnp.logical_and(n > 0, live))
        def _steady_step():
            z = in_proj()
            mlp_state = mlp_begin(x1_buf[...], n_ff - 1)
            gates = gate_proj(z)
            mlp_finish(*mlp_state)
            x1_buf[...] = mixer_rest(z, *gates)

        @pl.when(n == n_tiles)
        def _last_step():
            mlp_finish(*mlp_begin(x1_buf[...], 0))

        pl.when(live)(store_state)
    else:
        z = in_proj()
        x1 = mixer_rest(z, *gate_proj(z))
        if emit_y:
            mlp_finish(*mlp_begin(x1, 0))
        store_state()


def _const_spec(arr):
    nd = arr.ndim
    return pl.BlockSpec(arr.shape, lambda n, _nd=nd: (0,) * _nd, pipeline_mode=pl.Buffered(1))


def _run_layer(x, conv0, h0, dw0, params, *, n_streams, tm, emit_y, final_norm):
    B, T, D = x.shape
    S, TM = n_streams, tm
    assert B % S == 0 and T % TM == 0 and TM % SUBLANES == 0
    rnn_conv_w, d_rnn = params["cw"].shape
    dw_conv_w, d_conv = params["dww"].shape
    assert d_rnn % MXU_DIM == 0 and d_conv % LANES == 0
    cpad = -(-(rnn_conv_w - 1) // SUBLANES) * SUBLANES
    vpad = -(-(dw_conv_w - 1) // SUBLANES) * SUBLANES
    conv_rows = min(64, TM)
    assert TM % conv_rows == 0 and conv_rows % (2 * SUBLANES) == 0
    d_ff = params["w_up"].shape[-1]
    ff_chunk = min(1024, d_ff)
    tps = T // TM
    n_tiles = (B // S) * tps
    pipelined = emit_y and n_tiles > 1

    def in_tile(n):
        return jnp.minimum(n, n_tiles - 1) if pipelined else n

    def out_tile(n):
        return jnp.maximum(n - 1, 0) if pipelined else n

    def state_spec(arr):
        blk = (S,) + arr.shape[1:]
        if arr.shape[0] == B:
            return pl.BlockSpec(blk, lambda n: (in_tile(n) // tps, 0, 0))
        assert arr.shape[0] == 1 and S == 1
        return pl.BlockSpec(blk, lambda n: (0, 0, 0))

    order = ["norm_mix", "w_in", "cw", "cb", "w_gate", "b_r", "b_i", "lam", "dww", "dwb", "ln_g", "ln_b",
             "onr", "onc", "w_out", "norm_mlp", "w_up", "w_down", "norm_final"]
    weights = [params[k] for k in order]
    in_specs = [pl.BlockSpec((S, TM, D), lambda n: (in_tile(n) // tps, in_tile(n) % tps, 0)),
                state_spec(conv0), state_spec(h0), state_spec(dw0)]
    mlp_only = ("w_out", "norm_mlp", "w_up", "w_down", "norm_final")
    in_specs += [_const_spec(w) if emit_y or k not in mlp_only else pl.BlockSpec(memory_space=pl.ANY)
                 for k, w in zip(order, weights)]

    f32 = jnp.float32
    out_shape = [jax.ShapeDtypeStruct((B, rnn_conv_w - 1, d_rnn), f32),
                 jax.ShapeDtypeStruct((B, 1, d_rnn), f32),
                 jax.ShapeDtypeStruct((B, dw_conv_w - 1, d_conv), f32)]
    out_specs = [pl.BlockSpec((S, rnn_conv_w - 1, d_rnn), lambda n: (in_tile(n) // tps, 0, 0)),
                 pl.BlockSpec((S, 1, d_rnn), lambda n: (in_tile(n) // tps, 0, 0)),
                 pl.BlockSpec((S, dw_conv_w - 1, d_conv), lambda n: (in_tile(n) // tps, 0, 0))]
    if emit_y:
        out_shape = [jax.ShapeDtypeStruct((B, T, D), x.dtype)] + out_shape
        out_specs = [pl.BlockSpec((S, TM, D), lambda n: (out_tile(n) // tps, out_tile(n) % tps, 0))] + out_specs

    M = S * TM
    lvl_rows = _scan_level_rows(TM)
    scratch_shapes = [
        pltpu.VMEM((S, d_rnn // LANES, cpad + TM, LANES), f32),
        pltpu.VMEM((S, d_conv // LANES, vpad + TM, LANES), f32),
        pltpu.VMEM((S, SUBLANES, d_rnn), f32),
        pltpu.VMEM((d_rnn // LANES, M, LANES), f32),
        pltpu.VMEM((d_rnn // LANES, M, LANES), f32),
        pltpu.VMEM((d_rnn // LANES, M, LANES), f32),
        pltpu.VMEM((2, S, d_rnn // LANES, max(sum(lvl_rows), SUBLANES), LANES), f32),
        pltpu.VMEM((S, d_rnn // LANES, max(sum(r + 2 * SUBLANES for r in lvl_rows), SUBLANES), LANES), f32),
        pltpu.VMEM((d_rnn // LANES, M, LANES), f32),
        pltpu.VMEM((d_conv // LANES, M, LANES), f32),
    ]
    if pipelined:
        scratch_shapes.append(pltpu.VMEM((M, D), f32))
    kern = functools.partial(_layer_kernel, n_streams=S, tm=TM, n_tiles=n_tiles, tiles_per_stream=tps,
                             emit_y=emit_y, final_norm=final_norm, pipelined=pipelined,
                             rnn_conv_w=rnn_conv_w, dw_conv_w=dw_conv_w, ff_chunk=ff_chunk, conv_rows=conv_rows)
    return pl.pallas_call(
        kern,
        grid=(n_tiles + (1 if pipelined else 0),),
        in_specs=in_specs,
        out_specs=out_specs,
        out_shape=out_shape,
        scratch_shapes=scratch_shapes,
        compiler_params=pltpu.CompilerParams(
            dimension_semantics=("arbitrary",),
            vmem_limit_bytes=VMEM_LIMIT_BYTES),
        name=f"hybrid_layer_s{S}_t{TM}",
    )(x, conv0, h0, dw0, *weights)


def _block_diag_gates(w_r, w_i):
    n_heads, hd, _ = w_r.shape
    per = MXU_DIM // hd
    n_slab = n_heads // per
    on_diag = jnp.eye(per, dtype=bool)[None, :, None, :, None]

    def slabs(w):
        w = w.reshape(n_slab, per, hd, hd)
        bd = jnp.where(on_diag, w[:, :, :, None, :], 0.0)
        return bd.reshape(n_slab, MXU_DIM, MXU_DIM)

    return jnp.concatenate([slabs(w_r), slabs(w_i)], axis=-1).astype(jnp.bfloat16)


def kernel(x_prompt, x_sample, state_rglru_conv, state_rglru_h, state_dwconv, meta_tokens, norm_mix, w_in,
           rnn_conv_w, rnn_conv_b, w_gate_r, b_gate_r, w_gate_i, b_gate_i, rglru_lambda, dw_w, dw_b, ln_conv_g,
           ln_conv_b, out_norm_rnn, out_norm_conv, w_out, norm_mlp, w_up, w_down, norm_final):
    depth = norm_mix.shape[0]
    bf16 = jnp.bfloat16
    d_rnn = rnn_conv_w.shape[-1]
    d_conv = dw_w.shape[-1]
    c_hist = rnn_conv_w.shape[1] - 1
    v_hist = dw_w.shape[1] - 1

    xm = meta_tokens[None].astype(x_prompt.dtype)
    xp, xs = x_prompt, x_sample
    outs_p, outs_s = [], []
    for l in range(depth):
        last = l == depth - 1
        params = dict(
            norm_mix=norm_mix[l][None], w_in=w_in[l].astype(bf16), cw=rnn_conv_w[l], cb=rnn_conv_b[l][None],
            w_gate=_block_diag_gates(w_gate_r[l], w_gate_i[l]), b_r=b_gate_r[l][None], b_i=b_gate_i[l][None],
            lam=rglru_lambda[l][None], dww=dw_w[l], dwb=dw_b[l][None], ln_g=ln_conv_g[l][None],
            ln_b=ln_conv_b[l][None], onr=out_norm_rnn[l][None], onc=out_norm_conv[l][None],
            w_out=w_out[l].astype(bf16), norm_mlp=norm_mlp[l][None], w_up=w_up[l].astype(bf16),
            w_down=w_down[l].astype(bf16), norm_final=norm_final[None])
        f32 = jnp.float32
        zero_state = (jnp.zeros((1, c_hist, d_rnn), f32), jnp.zeros((1, 1, d_rnn), f32),
                      jnp.zeros((1, v_hist, d_conv), f32))
        res_m = _run_layer(xm, *zero_state, params, n_streams=1, tm=xm.shape[1], emit_y=not last,
                           final_norm=False)
        if not last:
            xm, res_m = res_m[0], res_m[1:]
        res_p = _run_layer(xp, *res_m, params, n_streams=1, tm=min(512, xp.shape[1]), emit_y=True,
                           final_norm=last)
        res_s = _run_layer(xs, state_rglru_conv[l], state_rglru_h[l][:, None], state_dwconv[l], params,
                           n_streams=xs.shape[0], tm=xs.shape[1], emit_y=True, final_norm=last)
        xp, xs = res_p[0], res_s[0]
        outs_p.append(res_p[1:])
        outs_s.append(res_s[1:])

    def stack(outs, i):
        return outs[0][i][None] if depth == 1 else jnp.stack([o[i] for o in outs])

    return (xp, xs,
            stack(outs_p, 0), stack(outs_p, 1)[:, :, 0], stack(outs_p, 2),
            stack(outs_s, 0), stack(outs_s, 1)[:, :, 0], stack(outs_s, 2))
```

```python
import functools
import math

import jax
import jax.numpy as jnp
from jax import lax
from jax.experimental import pallas as pl
from jax.experimental.pallas import tpu as pltpu

EPS = 1e-6
RGLRU_C = 8.0
SUBLANES = 8
LANES = 128
MXU_DIM = 256
VMEM_LIMIT_BYTES = 60000 * 1024


def _rms(x, g):
    ms = jnp.mean(x * x, axis=-1, keepdims=True)
    return x * lax.rsqrt(ms + EPS) * g


def _sigmoid(x):
    return 0.5 * jnp.tanh(0.5 * x) + 0.5


def _silu(x):
    h = 0.5 * x
    return h * (jnp.tanh(h) + 1.0)


def _gelu_tanh(x):
    c = math.sqrt(2.0 / math.pi)
    inner = x * (c + (0.044715 * c) * (x * x))
    return (0.5 * x) * (1.0 + jnp.tanh(inner))


def _bf16_dot(a, b):
    return jnp.dot(a.astype(jnp.bfloat16), b, preferred_element_type=jnp.float32)


def _to_slabs(dst, idx, x):
    for c in range(x.shape[-1] // LANES):
        dst[idx + (c,)] = x[:, c * LANES:(c + 1) * LANES]


def _from_slabs(buf):
    return jnp.concatenate([buf[c] for c in range(buf.shape[0])], axis=-1)


def _strided_causal_conv(win, s, w_ref, b_ref, out, *, taps, lead, tm, rows):
    half = rows // 2
    for c in range(win.shape[1]):
        lanes = slice(c * LANES, (c + 1) * LANES)
        for t0 in [i * rows + par for i in range(tm // rows) for par in range(2)]:
            acc = win[s, c, pl.ds(lead + t0, half, stride=2), :] * w_ref[0:1, lanes]
            for k in range(1, taps):
                acc = acc + win[s, c, pl.ds(lead + t0 + k, half, stride=2), :] * w_ref[k:k + 1, lanes]
            out[c, pl.ds(s * tm + t0, half, stride=2), :] = acc + b_ref[:, lanes]


def _scan_radices(n_rows):
    n_groups, radices = n_rows // SUBLANES, []
    while n_groups > 1:
        assert n_groups % 2 == 0
        r = 4 if n_groups % 4 == 0 else 2
        radices.append(r)
        n_groups //= r
    return radices


def _scan_level_rows(n_rows):
    rows, out = n_rows, []
    for r in _scan_radices(n_rows)[:-1]:
        rows //= r
        out.append(rows)
    return out


def _linear_scan_slab(a_ref, b_ref, h_ref, ab_lvl, hin_lvl, base, n_rows, carry):
    radices = _scan_radices(n_rows)
    lvl_rows = _scan_level_rows(n_rows)
    ab_off = [sum(lvl_rows[:i]) for i in range(len(lvl_rows))]
    hin_off = [sum(r + 2 * SUBLANES for r in lvl_rows[:i]) for i in range(len(lvl_rows))]

    partial = []
    rows = n_rows
    for lvl, r in enumerate(radices):
        g = rows // r
        src, off = ((a_ref, b_ref), base) if lvl == 0 else ((ab_lvl.at[0], ab_lvl.at[1]), ab_off[lvl - 1])
        ld = lambda kind, k: src[kind][pl.ds(off + k, g, stride=r), :]
        pa, pb = ld(0, 0), ld(1, 0)
        maps = [(pa, pb)]
        for k in range(1, r):
            ak, bk = ld(0, k), ld(1, k)
            pa, pb = ak * pa, ak * pb + bk
            maps.append((pa, pb))
        partial.append(maps)
        rows = g
        if lvl + 1 < len(radices):
            ab_lvl[0, ab_off[lvl]:ab_off[lvl] + g, :] = pa
            ab_lvl[1, ab_off[lvl]:ab_off[lvl] + g, :] = pb
    if not radices:
        pa, pb = a_ref[base:base + SUBLANES, :], b_ref[base:base + SUBLANES, :]

    row = lax.broadcasted_iota(jnp.int32, (SUBLANES, LANES), 0)
    for sh in (1, 2, 4):
        keep = row >= sh
        pa_s = jnp.where(keep, pltpu.roll(pa, sh, 0), 1.0)
        pb_s = jnp.where(keep, pltpu.roll(pb, sh, 0), 0.0)
        pa, pb = pa * pa_s, pa * pb_s + pb
    h_top = pa * carry + pb
    new_carry = jnp.broadcast_to(h_top[SUBLANES - 1:SUBLANES, :], (SUBLANES, LANES))
    if not radices:
        h_ref[base:base + SUBLANES, :] = h_top
        return new_carry
    h_prev = jnp.where(row == 0, carry, pltpu.roll(h_top, 1, 0))

    for lvl in reversed(range(len(radices))):
        r = radices[lvl]
        g = partial[lvl][0][0].shape[0]
        if lvl < len(radices) - 1:
            o = hin_off[lvl] + SUBLANES
            h_prev = hin_lvl[o:o + g, :]
        for k, (pa, pb) in enumerate(partial[lvl]):
            hk = pa * h_prev + pb
            if lvl == 0:
                h_ref[pl.ds(base + k, g, stride=r), :] = hk
            else:
                o = hin_off[lvl - 1] + SUBLANES
                hin_lvl[pl.ds(o + 1 + k, g, stride=r), :] = hk
        if lvl > 0:
            o = hin_off[lvl - 1] + SUBLANES
            hin_lvl[o:o + 1, :] = carry[0:1, :]
    return new_carry


def _layer_kernel(x_ref, conv0_ref, h0_ref, dw0_ref,
                  norm_mix_ref, w_in_ref, cw_ref, cb_ref, w_gate_ref, b_r_ref, b_i_ref, lam_ref,
                  dww_ref, dwb_ref, ln_g_ref, ln_b_ref, onr_ref, onc_ref,
                  w_out_ref, norm_mlp_ref, w_up_ref, w_down_ref, norm_final_ref,
                  *rest, n_streams, tm, n_tiles, tiles_per_stream, emit_y, final_norm, pipelined,
                  rnn_conv_w, dw_conv_w, ff_chunk, conv_rows):
    if emit_y:
        y_ref, conv_out_ref, h_out_ref, dw_out_ref = rest[:4]
        scratch = rest[4:]
    else:
        conv_out_ref, h_out_ref, dw_out_ref = rest[:3]
        scratch = rest[3:]
    cbuf, vbuf, hcar, a_buf, b_buf, h_buf, ab_lvl, hin_lvl, xc_buf, vc_buf = scratch[:10]

    w_hbm = dict(w_in=w_in_ref, w_gate=w_gate_ref, w_out=w_out_ref, w_up=w_up_ref, w_down=w_down_ref)
    w_names = _matmul_weights(emit_y)
    w_vmem = dict(zip(w_names, scratch[10:10 + len(w_names)]))
    w_sem = scratch[10 + len(w_names)]
    scratch = scratch[:10] + scratch[11 + len(w_names):]
    in_flight = set()

    def weight_copy(name):
        return pltpu.make_async_copy(w_hbm[name], w_vmem[name], w_sem.at[w_names.index(name)])

    def start_weight_copies():
        for name in w_names:
            weight_copy(name).start()
            in_flight.add(name)

    def finish_weight_copies():
        for name in w_names:
            weight(name)

    def weight(name):
        if name in in_flight:
            weight_copy(name).wait()
            in_flight.discard(name)
        return w_vmem[name]

    S, TM = n_streams, tm
    M = S * TM
    d_rnn = cw_ref.shape[-1]
    d_conv = dww_ref.shape[-1]
    cpad = cbuf.shape[2] - TM
    vpad = vbuf.shape[2] - TM
    c_hist = rnn_conv_w - 1
    v_hist = dw_conv_w - 1
    n = pl.program_id(0)
    tile = jnp.minimum(n, n_tiles - 1)
    live = n < n_tiles

    @pl.when(jnp.logical_and(lax.rem(tile, tiles_per_stream) == 0, live))
    def _load_state():
        cbuf[:, :, 0:cpad - c_hist, :] = jnp.zeros((S, d_rnn // LANES, cpad - c_hist, LANES), jnp.float32)
        vbuf[:, :, 0:vpad - v_hist, :] = jnp.zeros((S, d_conv // LANES, vpad - v_hist, LANES), jnp.float32)
        for c in range(d_rnn // LANES):
            cbuf[:, c, cpad - c_hist:cpad, :] = conv0_ref[:, :, c * LANES:(c + 1) * LANES]
        for c in range(d_conv // LANES):
            vbuf[:, c, vpad - v_hist:vpad, :] = dw0_ref[:, :, c * LANES:(c + 1) * LANES]
        hcar[...] = jnp.broadcast_to(h0_ref[...], hcar.shape)

    def in_proj():
        x = x_ref[...].reshape(M, x_ref.shape[-1])
        hn = _rms(x, norm_mix_ref[...])
        return _bf16_dot(hn, weight("w_in")[...])

    def gate_proj(z):
        xr = z[:, 0:d_rnn]
        for s in range(S):
            _to_slabs(cbuf.at[:, :, cpad:cpad + TM, :], (s,), xr[s * TM:(s + 1) * TM, :])
            _strided_causal_conv(cbuf, s, cw_ref, cb_ref, xc_buf, taps=rnn_conv_w, lead=cpad - c_hist, tm=TM,
                                 rows=conv_rows)
            for c in range(d_rnn // LANES):
                cbuf[s, c, 0:cpad, :] = cbuf[s, c, TM:TM + cpad, :]
        xr_c = _from_slabs(xc_buf)

        xb = xr_c.astype(jnp.bfloat16)
        r_parts, i_parts = [], []
        for j in range(d_rnn // MXU_DIM):
            ri = jnp.dot(xb[:, j * MXU_DIM:(j + 1) * MXU_DIM], weight("w_gate")[j],
                         preferred_element_type=jnp.float32)
            r_parts.append(ri[:, :MXU_DIM])
            i_parts.append(ri[:, MXU_DIM:])
        return xr_c, r_parts, i_parts

    def mixer_rest(z, xr_c, r_parts, i_parts):
        x = x_ref[...].reshape(M, x_ref.shape[-1])
        gate = z[:, d_rnn:2 * d_rnn]
        glu_v = z[:, 2 * d_rnn:2 * d_rnn + d_conv]
        glu_g = z[:, 2 * d_rnn + d_conv:]
        r = _sigmoid(jnp.concatenate(r_parts, axis=-1) + b_r_ref[...])
        ig = _sigmoid(jnp.concatenate(i_parts, axis=-1) + b_i_ref[...])
        nl = -lam_ref[...]
        softplus_nl = jnp.maximum(nl, 0.0) + jnp.log1p(jnp.exp(-jnp.abs(nl)))
        log_a = (-RGLRU_C * r) * softplus_nl
        a = jnp.exp(log_a)
        mult = jnp.sqrt(jnp.maximum(1.0 - a * a, 0.0))
        _to_slabs(a_buf, (), a)
        _to_slabs(b_buf, (), mult * ig * xr_c)
        for s in range(S):
            carry = hcar[s]
            hcar[s] = jnp.concatenate(
                [_linear_scan_slab(a_buf.at[c], b_buf.at[c], h_buf.at[c], ab_lvl.at[:, s, c], hin_lvl.at[s, c],
                                   s * TM, TM, carry[:, c * LANES:(c + 1) * LANES])
                 for c in range(d_rnn // LANES)], axis=-1)

        v = glu_v * _sigmoid(glu_g)
        for s in range(S):
            _to_slabs(vbuf.at[:, :, vpad:vpad + TM, :], (s,), v[s * TM:(s + 1) * TM, :])
            _strided_causal_conv(vbuf, s, dww_ref, dwb_ref, vc_buf, taps=dw_conv_w, lead=vpad - v_hist, tm=TM,
                                 rows=conv_rows)
            for c in range(d_conv // LANES):
                vbuf[s, c, 0:vpad, :] = vbuf[s, c, TM:TM + vpad, :]
        if not emit_y:
            return None

        y_rnn = _from_slabs(h_buf) * _gelu_tanh(gate)
        vc = _from_slabs(vc_buf)
        mu = jnp.mean(vc, axis=-1, keepdims=True)
        vcc = vc - mu
        ln = vcc * lax.rsqrt(jnp.mean(vcc * vcc, axis=-1, keepdims=True) + EPS) * ln_g_ref[...] + ln_b_ref[...]
        y_conv = _silu(ln)
        mix = jnp.concatenate([_rms(y_rnn, onr_ref[...]), _rms(y_conv, onc_ref[...])], axis=-1)
        return x + _bf16_dot(mix, weight("w_out")[...])

    n_ff = w_up_ref.shape[-1] // ff_chunk

    def mlp_chunk(hm, x2, c):
        hc = jnp.dot(hm, weight("w_up")[:, c * ff_chunk:(c + 1) * ff_chunk], preferred_element_type=jnp.float32)
        hc = jnp.square(jnp.maximum(hc, 0.0))
        return x2 + _bf16_dot(hc, weight("w_down")[c * ff_chunk:(c + 1) * ff_chunk, :])

    def mlp_begin(x1, n_chunks):
        hm = _rms(x1, norm_mlp_ref[...]).astype(jnp.bfloat16)
        x2 = x1
        for c in range(n_chunks):
            x2 = mlp_chunk(hm, x2, c)
        return hm, x2, n_chunks

    def mlp_finish(hm, x2, n_done):
        for c in range(n_done, n_ff):
            x2 = mlp_chunk(hm, x2, c)
        if final_norm:
            x2 = _rms(x2, norm_final_ref[...])
        y_ref[...] = x2.reshape(y_ref.shape)

    def store_state():
        for s in range(S):
            for c in range(d_rnn // LANES):
                conv_out_ref[s, :, c * LANES:(c + 1) * LANES] = cbuf[s, c, cpad - c_hist:cpad, :]
            for c in range(d_conv // LANES):
                dw_out_ref[s, :, c * LANES:(c + 1) * LANES] = vbuf[s, c, vpad - v_hist:vpad, :]
            h_out_ref[s] = hcar[s, 0:1, :]

    if pipelined:
        x1_buf = scratch[10]

        @pl.when(n == 0)
        def _first_step():
            start_weight_copies()
            z = in_proj()
            x1_buf[...] = mixer_rest(z, *gate_proj(z))
            finish_weight_copies()

        @pl.when(jnp.logical_and(n > 0, live))
        def _steady_step():
            z = in_proj()
            mlp_state = mlp_begin(x1_buf[...], n_ff - 1)
            gates = gate_proj(z)
            mlp_finish(*mlp_state)
            x1_buf[...] = mixer_rest(z, *gates)

        @pl.when(n == n_tiles)
        def _last_step():
            mlp_finish(*mlp_begin(x1_buf[...], 0))

        pl.when(live)(store_state)
    else:
        if n_tiles == 1:
            start_weight_copies()
        else:
            @pl.when(n == 0)
            def _load_weights():
                start_weight_copies()
                finish_weight_copies()
        z = in_proj()
        x1 = mixer_rest(z, *gate_proj(z))
        if emit_y:
            mlp_finish(*mlp_begin(x1, 0))
        finish_weight_copies()
        store_state()


def _matmul_weights(emit_y):
    return ("w_in", "w_gate") + (("w_out", "w_up", "w_down") if emit_y else ())


def _const_spec(arr):
    nd = arr.ndim
    return pl.BlockSpec(arr.shape, lambda n, _nd=nd: (0,) * _nd, pipeline_mode=pl.Buffered(1))


def _run_layer(x, conv0, h0, dw0, params, *, n_streams, tm, emit_y, final_norm):
    B, T, D = x.shape
    S, TM = n_streams, tm
    assert B % S == 0 and T % TM == 0 and TM % SUBLANES == 0
    rnn_conv_w, d_rnn = params["cw"].shape
    dw_conv_w, d_conv = params["dww"].shape
    assert d_rnn % MXU_DIM == 0 and d_conv % LANES == 0
    cpad = -(-(rnn_conv_w - 1) // SUBLANES) * SUBLANES
    vpad = -(-(dw_conv_w - 1) // SUBLANES) * SUBLANES
    conv_rows = min(64, TM)
    assert TM % conv_rows == 0 and conv_rows % (2 * SUBLANES) == 0
    d_ff = params["w_up"].shape[-1]
    ff_chunk = min(1024, d_ff)
    tps = T // TM
    n_tiles = (B // S) * tps
    pipelined = emit_y and n_tiles > 1

    def in_tile(n):
        return jnp.minimum(n, n_tiles - 1) if pipelined else n

    def out_tile(n):
        return jnp.maximum(n - 1, 0) if pipelined else n

    def state_spec(arr):
        blk = (S,) + arr.shape[1:]
        if arr.shape[0] == B:
            return pl.BlockSpec(blk, lambda n: (in_tile(n) // tps, 0, 0))
        assert arr.shape[0] == 1 and S == 1
        return pl.BlockSpec(blk, lambda n: (0, 0, 0))

    order = ["norm_mix", "w_in", "cw", "cb", "w_gate", "b_r", "b_i", "lam", "dww", "dwb", "ln_g", "ln_b",
             "onr", "onc", "w_out", "norm_mlp", "w_up", "w_down", "norm_final"]
    weights = [params[k] for k in order]
    in_specs = [pl.BlockSpec((S, TM, D), lambda n: (in_tile(n) // tps, in_tile(n) % tps, 0)),
                state_spec(conv0), state_spec(h0), state_spec(dw0)]
    in_hbm = ("w_in", "w_gate", "w_out", "w_up", "w_down")
    in_specs += [pl.BlockSpec(memory_space=pl.ANY) if k in in_hbm else _const_spec(w) for k, w in zip(order, weights)]

    f32 = jnp.float32
    out_shape = [jax.ShapeDtypeStruct((B, rnn_conv_w - 1, d_rnn), f32),
                 jax.ShapeDtypeStruct((B, 1, d_rnn), f32),
                 jax.ShapeDtypeStruct((B, dw_conv_w - 1, d_conv), f32)]
    out_specs = [pl.BlockSpec((S, rnn_conv_w - 1, d_rnn), lambda n: (in_tile(n) // tps, 0, 0)),
                 pl.BlockSpec((S, 1, d_rnn), lambda n: (in_tile(n) // tps, 0, 0)),
                 pl.BlockSpec((S, dw_conv_w - 1, d_conv), lambda n: (in_tile(n) // tps, 0, 0))]
    if emit_y:
        out_shape = [jax.ShapeDtypeStruct((B, T, D), x.dtype)] + out_shape
        out_specs = [pl.BlockSpec((S, TM, D), lambda n: (out_tile(n) // tps, out_tile(n) % tps, 0))] + out_specs

    M = S * TM
    lvl_rows = _scan_level_rows(TM)
    scratch_shapes = [
        pltpu.VMEM((S, d_rnn // LANES, cpad + TM, LANES), f32),
        pltpu.VMEM((S, d_conv // LANES, vpad + TM, LANES), f32),
        pltpu.VMEM((S, SUBLANES, d_rnn), f32),
        pltpu.VMEM((d_rnn // LANES, M, LANES), f32),
        pltpu.VMEM((d_rnn // LANES, M, LANES), f32),
        pltpu.VMEM((d_rnn // LANES, M, LANES), f32),
        pltpu.VMEM((2, S, d_rnn // LANES, max(sum(lvl_rows), SUBLANES), LANES), f32),
        pltpu.VMEM((S, d_rnn // LANES, max(sum(r + 2 * SUBLANES for r in lvl_rows), SUBLANES), LANES), f32),
        pltpu.VMEM((d_rnn // LANES, M, LANES), f32),
        pltpu.VMEM((d_conv // LANES, M, LANES), f32),
    ]
    w_names = _matmul_weights(emit_y)
    scratch_shapes += [pltpu.VMEM(params[k].shape, params[k].dtype) for k in w_names]
    scratch_shapes.append(pltpu.SemaphoreType.DMA((len(w_names),)))
    if pipelined:
        scratch_shapes.append(pltpu.VMEM((M, D), f32))
    kern = functools.partial(_layer_kernel, n_streams=S, tm=TM, n_tiles=n_tiles, tiles_per_stream=tps,
                             emit_y=emit_y, final_norm=final_norm, pipelined=pipelined,
                             rnn_conv_w=rnn_conv_w, dw_conv_w=dw_conv_w, ff_chunk=ff_chunk, conv_rows=conv_rows)
    return pl.pallas_call(
        kern,
        grid=(n_tiles + (1 if pipelined else 0),),
        in_specs=in_specs,
        out_specs=out_specs,
        out_shape=out_shape,
        scratch_shapes=scratch_shapes,
        compiler_params=pltpu.CompilerParams(
            dimension_semantics=("arbitrary",),
            vmem_limit_bytes=VMEM_LIMIT_BYTES),
        name=f"hybrid_layer_s{S}_t{TM}",
    )(x, conv0, h0, dw0, *weights)


def _block_diag_gates(w_r, w_i):
    n_heads, hd, _ = w_r.shape
    per = MXU_DIM // hd
    n_slab = n_heads // per
    on_diag = jnp.eye(per, dtype=bool)[None, :, None, :, None]

    def slabs(w):
        w = w.reshape(n_slab, per, hd, hd)
        bd = jnp.where(on_diag, w[:, :, :, None, :], 0.0)
        return bd.reshape(n_slab, MXU_DIM, MXU_DIM)

    return jnp.concatenate([slabs(w_r), slabs(w_i)], axis=-1).astype(jnp.bfloat16)


def kernel(x_prompt, x_sample, state_rglru_conv, state_rglru_h, state_dwconv, meta_tokens, norm_mix, w_in,
           rnn_conv_w, rnn_conv_b, w_gate_r, b_gate_r, w_gate_i, b_gate_i, rglru_lambda, dw_w, dw_b, ln_conv_g,
           ln_conv_b, out_norm_rnn, out_norm_conv, w_out, norm_mlp, w_up, w_down, norm_final):
    depth = norm_mix.shape[0]
    bf16 = jnp.bfloat16
    d_rnn = rnn_conv_w.shape[-1]
    d_conv = dw_w.shape[-1]
    c_hist = rnn_conv_w.shape[1] - 1
    v_hist = dw_w.shape[1] - 1

    xm = meta_tokens[None].astype(x_prompt.dtype)
    xp, xs = x_prompt, x_sample
    outs_p, outs_s = [], []
    for l in range(depth):
        last = l == depth - 1
        params = dict(
            norm_mix=norm_mix[l][None], w_in=w_in[l].astype(bf16), cw=rnn_conv_w[l], cb=rnn_conv_b[l][None],
            w_gate=_block_diag_gates(w_gate_r[l], w_gate_i[l]), b_r=b_gate_r[l][None], b_i=b_gate_i[l][None],
            lam=rglru_lambda[l][None], dww=dw_w[l], dwb=dw_b[l][None], ln_g=ln_conv_g[l][None],
            ln_b=ln_conv_b[l][None], onr=out_norm_rnn[l][None], onc=out_norm_conv[l][None],
            w_out=w_out[l].astype(bf16), norm_mlp=norm_mlp[l][None], w_up=w_up[l].astype(bf16),
            w_down=w_down[l].astype(bf16), norm_final=norm_final[None])
        f32 = jnp.float32
        zero_state = (jnp.zeros((1, c_hist, d_rnn), f32), jnp.zeros((1, 1, d_rnn), f32),
                      jnp.zeros((1, v_hist, d_conv), f32))
        res_m = _run_layer(xm, *zero_state, params, n_streams=1, tm=xm.shape[1], emit_y=not last,
                           final_norm=False)
        if not last:
            xm, res_m = res_m[0], res_m[1:]
        res_p = _run_layer(xp, *res_m, params, n_streams=1, tm=min(512, xp.shape[1]), emit_y=True,
                           final_norm=last)
        res_s = _run_layer(xs, state_rglru_conv[l], state_rglru_h[l][:, None], state_dwconv[l], params,
                           n_streams=xs.shape[0], tm=xs.shape[1], emit_y=True, final_norm=last)
        xp, xs = res_p[0], res_s[0]
        outs_p.append(res_p[1:])
        outs_s.append(res_s[1:])

    def stack(outs, i):
        return outs[0][i][None] if depth == 1 else jnp.stack([o[i] for o in outs])

    return (xp, xs,
            stack(outs_p, 0), stack(outs_p, 1)[:, :, 0], stack(outs_p, 2),
            stack(outs_s, 0), stack(outs_s, 1)[:, :, 0], stack(outs_s, 2))
```

```python
import functools
import math

import jax
import jax.numpy as jnp
from jax import lax
from jax.experimental import pallas as pl
from jax.experimental.pallas import tpu as pltpu

EPS = 1e-6
RGLRU_C = 8.0
SUBLANES = 8
LANES = 128
MXU_DIM = 256
VMEM_LIMIT_BYTES = 60000 * 1024
CAST_CHUNK_BYTES = 1 << 20


def _rms(x, g):
    ms = jnp.mean(x * x, axis=-1, keepdims=True)
    return x * lax.rsqrt(ms + EPS) * g


def _sigmoid(x):
    return 0.5 * jnp.tanh(0.5 * x) + 0.5


def _silu(x):
    h = 0.5 * x
    return h * (jnp.tanh(h) + 1.0)


def _gelu_tanh(x):
    c = math.sqrt(2.0 / math.pi)
    inner = x * (c + (0.044715 * c) * (x * x))
    return (0.5 * x) * (1.0 + jnp.tanh(inner))


def _bf16_dot(a, b):
    return jnp.dot(a.astype(jnp.bfloat16), b, preferred_element_type=jnp.float32)


def _to_slabs(dst, idx, x):
    for c in range(x.shape[-1] // LANES):
        dst[idx + (c,)] = x[:, c * LANES:(c + 1) * LANES]


def _from_slabs(buf):
    return jnp.concatenate([buf[c] for c in range(buf.shape[0])], axis=-1)


def _strided_causal_conv(win, s, w_ref, b_ref, out, *, taps, lead, tm, rows):
    half = rows // 2
    for c in range(win.shape[1]):
        lanes = slice(c * LANES, (c + 1) * LANES)
        for t0 in [i * rows + par for i in range(tm // rows) for par in range(2)]:
            acc = win[s, c, pl.ds(lead + t0, half, stride=2), :] * w_ref[0:1, lanes]
            for k in range(1, taps):
                acc = acc + win[s, c, pl.ds(lead + t0 + k, half, stride=2), :] * w_ref[k:k + 1, lanes]
            out[c, pl.ds(s * tm + t0, half, stride=2), :] = acc + b_ref[:, lanes]


def _scan_radices(n_rows):
    n_groups, radices = n_rows // SUBLANES, []
    while n_groups > 1:
        assert n_groups % 2 == 0
        r = 4 if n_groups % 4 == 0 else 2
        radices.append(r)
        n_groups //= r
    return radices


def _scan_level_rows(n_rows):
    rows, out = n_rows, []
    for r in _scan_radices(n_rows)[:-1]:
        rows //= r
        out.append(rows)
    return out


def _linear_scan_slab(a_ref, b_ref, h_ref, ab_lvl, hin_lvl, base, n_rows, carry):
    radices = _scan_radices(n_rows)
    lvl_rows = _scan_level_rows(n_rows)
    ab_off = [sum(lvl_rows[:i]) for i in range(len(lvl_rows))]
    hin_off = [sum(r + 2 * SUBLANES for r in lvl_rows[:i]) for i in range(len(lvl_rows))]

    partial = []
    rows = n_rows
    for lvl, r in enumerate(radices):
        g = rows // r
        src, off = ((a_ref, b_ref), base) if lvl == 0 else ((ab_lvl.at[0], ab_lvl.at[1]), ab_off[lvl - 1])
        ld = lambda kind, k: src[kind][pl.ds(off + k, g, stride=r), :]
        pa, pb = ld(0, 0), ld(1, 0)
        maps = [(pa, pb)]
        for k in range(1, r):
            ak, bk = ld(0, k), ld(1, k)
            pa, pb = ak * pa, ak * pb + bk
            maps.append((pa, pb))
        partial.append(maps)
        rows = g
        if lvl + 1 < len(radices):
            ab_lvl[0, ab_off[lvl]:ab_off[lvl] + g, :] = pa
            ab_lvl[1, ab_off[lvl]:ab_off[lvl] + g, :] = pb
    if not radices:
        pa, pb = a_ref[base:base + SUBLANES, :], b_ref[base:base + SUBLANES, :]

    row = lax.broadcasted_iota(jnp.int32, (SUBLANES, LANES), 0)
    for sh in (1, 2, 4):
        keep = row >= sh
        pa_s = jnp.where(keep, pltpu.roll(pa, sh, 0), 1.0)
        pb_s = jnp.where(keep, pltpu.roll(pb, sh, 0), 0.0)
        pa, pb = pa * pa_s, pa * pb_s + pb
    h_top = pa * carry + pb
    new_carry = jnp.broadcast_to(h_top[SUBLANES - 1:SUBLANES, :], (SUBLANES, LANES))
    if not radices:
        h_ref[base:base + SUBLANES, :] = h_top
        return new_carry
    h_prev = jnp.where(row == 0, carry, pltpu.roll(h_top, 1, 0))

    for lvl in reversed(range(len(radices))):
        r = radices[lvl]
        g = partial[lvl][0][0].shape[0]
        if lvl < len(radices) - 1:
            o = hin_off[lvl] + SUBLANES
            h_prev = hin_lvl[o:o + g, :]
        for k, (pa, pb) in enumerate(partial[lvl]):
            hk = pa * h_prev + pb
            if lvl == 0:
                h_ref[pl.ds(base + k, g, stride=r), :] = hk
            else:
                o = hin_off[lvl - 1] + SUBLANES
                hin_lvl[pl.ds(o + 1 + k, g, stride=r), :] = hk
        if lvl > 0:
            o = hin_off[lvl - 1] + SUBLANES
            hin_lvl[o:o + 1, :] = carry[0:1, :]
    return new_carry


def _layer_kernel(x_ref, conv0_ref, h0_ref, dw0_ref,
                  norm_mix_ref, w_in_ref, cw_ref, cb_ref, w_gate_ref, b_r_ref, b_i_ref, lam_ref,
                  dww_ref, dwb_ref, ln_g_ref, ln_b_ref, onr_ref, onc_ref,
                  w_out_ref, norm_mlp_ref, w_up_ref, w_down_ref, norm_final_ref,
                  *rest, n_streams, tm, n_tiles, tiles_per_stream, emit_y, final_norm, pipelined,
                  rnn_conv_w, dw_conv_w, ff_chunk, conv_rows, cast_names):
    n_out = 4 if emit_y else 3
    if emit_y:
        y_ref, conv_out_ref, h_out_ref, dw_out_ref = rest[:4]
    else:
        conv_out_ref, h_out_ref, dw_out_ref = rest[:3]
    w_bf16_out = dict(zip(cast_names, rest[n_out:n_out + len(cast_names)]))
    scratch = rest[n_out + len(cast_names):]
    cbuf, vbuf, hcar, a_buf, b_buf, h_buf, ab_lvl, hin_lvl, xc_buf, vc_buf = scratch[:10]

    w_hbm = dict(w_in=w_in_ref, w_gate=w_gate_ref, w_out=w_out_ref, w_up=w_up_ref, w_down=w_down_ref)
    w_names = _matmul_weights(emit_y)
    w_vmem = dict(zip(w_names, scratch[10:10 + len(w_names)]))
    w_sem = scratch[10 + len(w_names)]
    scratch = scratch[:10] + scratch[11 + len(w_names):]
    stage = dict(zip(cast_names, scratch[10:10 + len(cast_names)]))
    if cast_names:
        stage_sem, writeback_sem = scratch[10 + len(cast_names):12 + len(cast_names)]
        scratch = scratch[:10] + scratch[12 + len(cast_names):]
    in_flight = set()

    def weight_copy(name):
        return pltpu.make_async_copy(w_hbm[name], w_vmem[name], w_sem.at[w_names.index(name)])

    def start_weight_copies():
        for name in w_names:
            if name not in cast_names:
                weight_copy(name).start()
                in_flight.add(name)

    def cast_steps():
        items = [(name, c) for name in cast_names for c in range(w_hbm[name].shape[0] // stage[name].shape[1])]

        def fetch(name, c):
            rows = stage[name].shape[1]
            return pltpu.make_async_copy(w_hbm[name].at[pl.ds(c * rows, rows), :], stage[name].at[c % 2],
                                         stage_sem.at[cast_names.index(name), c % 2])

        def step(i):
            name, c = items[i]
            if i == 0:
                fetch(name, c).start()
            if i + 1 < len(items):
                fetch(*items[i + 1]).start()
            fetch(name, c).wait()
            rows = stage[name].shape[1]
            w_vmem[name][c * rows:(c + 1) * rows, :] = stage[name][c % 2].astype(jnp.bfloat16)

        return [functools.partial(step, i) for i in range(len(items))]

    def writeback(name):
        return pltpu.make_async_copy(w_vmem[name], w_bf16_out[name], writeback_sem.at[cast_names.index(name)])

    def finish_weight_copies():
        for name in w_names:
            weight(name)

    def weight(name):
        if name in in_flight:
            weight_copy(name).wait()
            in_flight.discard(name)
        return w_vmem[name]

    S, TM = n_streams, tm
    M = S * TM
    d_rnn = cw_ref.shape[-1]
    d_conv = dww_ref.shape[-1]
    cpad = cbuf.shape[2] - TM
    vpad = vbuf.shape[2] - TM
    c_hist = rnn_conv_w - 1
    v_hist = dw_conv_w - 1
    n = pl.program_id(0)
    tile = jnp.minimum(n, n_tiles - 1)
    live = n < n_tiles

    @pl.when(jnp.logical_and(lax.rem(tile, tiles_per_stream) == 0, live))
    def _load_state():
        cbuf[:, :, 0:cpad - c_hist, :] = jnp.zeros((S, d_rnn // LANES, cpad - c_hist, LANES), jnp.float32)
        vbuf[:, :, 0:vpad - v_hist, :] = jnp.zeros((S, d_conv // LANES, vpad - v_hist, LANES), jnp.float32)
        for c in range(d_rnn // LANES):
            cbuf[:, c, cpad - c_hist:cpad, :] = conv0_ref[:, :, c * LANES:(c + 1) * LANES]
        for c in range(d_conv // LANES):
            vbuf[:, c, vpad - v_hist:vpad, :] = dw0_ref[:, :, c * LANES:(c + 1) * LANES]
        hcar[...] = jnp.broadcast_to(h0_ref[...], hcar.shape)

    def in_proj():
        x = x_ref[...].reshape(M, x_ref.shape[-1])
        hn = _rms(x, norm_mix_ref[...])
        return _bf16_dot(hn, weight("w_in")[...])

    def gate_proj(z):
        xr = z[:, 0:d_rnn]
        for s in range(S):
            _to_slabs(cbuf.at[:, :, cpad:cpad + TM, :], (s,), xr[s * TM:(s + 1) * TM, :])
            _strided_causal_conv(cbuf, s, cw_ref, cb_ref, xc_buf, taps=rnn_conv_w, lead=cpad - c_hist, tm=TM,
                                 rows=conv_rows)
            for c in range(d_rnn // LANES):
                cbuf[s, c, 0:cpad, :] = cbuf[s, c, TM:TM + cpad, :]
        xr_c = _from_slabs(xc_buf)

        xb = xr_c.astype(jnp.bfloat16)
        r_parts, i_parts = [], []
        for j in range(d_rnn // MXU_DIM):
            ri = jnp.dot(xb[:, j * MXU_DIM:(j + 1) * MXU_DIM], weight("w_gate")[j],
                         preferred_element_type=jnp.float32)
            r_parts.append(ri[:, :MXU_DIM])
            i_parts.append(ri[:, MXU_DIM:])
        return xr_c, r_parts, i_parts

    def mixer_rest(z, xr_c, r_parts, i_parts):
        x = x_ref[...].reshape(M, x_ref.shape[-1])
        gate = z[:, d_rnn:2 * d_rnn]
        glu_v = z[:, 2 * d_rnn:2 * d_rnn + d_conv]
        glu_g = z[:, 2 * d_rnn + d_conv:]
        r = _sigmoid(jnp.concatenate(r_parts, axis=-1) + b_r_ref[...])
        ig = _sigmoid(jnp.concatenate(i_parts, axis=-1) + b_i_ref[...])
        nl = -lam_ref[...]
        softplus_nl = jnp.maximum(nl, 0.0) + jnp.log1p(jnp.exp(-jnp.abs(nl)))
        log_a = (-RGLRU_C * r) * softplus_nl
        a = jnp.exp(log_a)
        mult = jnp.sqrt(jnp.maximum(1.0 - a * a, 0.0))
        _to_slabs(a_buf, (), a)
        _to_slabs(b_buf, (), mult * ig * xr_c)
        for s in range(S):
            carry = hcar[s]
            hcar[s] = jnp.concatenate(
                [_linear_scan_slab(a_buf.at[c], b_buf.at[c], h_buf.at[c], ab_lvl.at[:, s, c], hin_lvl.at[s, c],
                                   s * TM, TM, carry[:, c * LANES:(c + 1) * LANES])
                 for c in range(d_rnn // LANES)], axis=-1)

        v = glu_v * _sigmoid(glu_g)
        for s in range(S):
            _to_slabs(vbuf.at[:, :, vpad:vpad + TM, :], (s,), v[s * TM:(s + 1) * TM, :])
            _strided_causal_conv(vbuf, s, dww_ref, dwb_ref, vc_buf, taps=dw_conv_w, lead=vpad - v_hist, tm=TM,
                                 rows=conv_rows)
            for c in range(d_conv // LANES):
                vbuf[s, c, 0:vpad, :] = vbuf[s, c, TM:TM + vpad, :]
        if not emit_y:
            return None

        y_rnn = _from_slabs(h_buf) * _gelu_tanh(gate)
        vc = _from_slabs(vc_buf)
        mu = jnp.mean(vc, axis=-1, keepdims=True)
        vcc = vc - mu
        ln = vcc * lax.rsqrt(jnp.mean(vcc * vcc, axis=-1, keepdims=True) + EPS) * ln_g_ref[...] + ln_b_ref[...]
        y_conv = _silu(ln)
        mix = jnp.concatenate([_rms(y_rnn, onr_ref[...]), _rms(y_conv, onc_ref[...])], axis=-1)
        return x + _bf16_dot(mix, weight("w_out")[...])

    n_ff = w_up_ref.shape[-1] // ff_chunk

    def mlp_chunk(hm, x2, c):
        hc = jnp.dot(hm, weight("w_up")[:, c * ff_chunk:(c + 1) * ff_chunk], preferred_element_type=jnp.float32)
        hc = jnp.square(jnp.maximum(hc, 0.0))
        return x2 + _bf16_dot(hc, weight("w_down")[c * ff_chunk:(c + 1) * ff_chunk, :])

    def mlp_begin(x1, n_chunks):
        hm = _rms(x1, norm_mlp_ref[...]).astype(jnp.bfloat16)
        x2 = x1
        for c in range(n_chunks):
            x2 = mlp_chunk(hm, x2, c)
        return hm, x2, n_chunks

    def mlp_finish(hm, x2, n_done):
        for c in range(n_done, n_ff):
            x2 = mlp_chunk(hm, x2, c)
        if final_norm:
            x2 = _rms(x2, norm_final_ref[...])
        y_ref[...] = x2.reshape(y_ref.shape)

    def store_state():
        for s in range(S):
            for c in range(d_rnn // LANES):
                conv_out_ref[s, :, c * LANES:(c + 1) * LANES] = cbuf[s, c, cpad - c_hist:cpad, :]
            for c in range(d_conv // LANES):
                dw_out_ref[s, :, c * LANES:(c + 1) * LANES] = vbuf[s, c, vpad - v_hist:vpad, :]
            h_out_ref[s] = hcar[s, 0:1, :]

    if pipelined:
        x1_buf = scratch[10]

        @pl.when(n == 0)
        def _first_step():
            start_weight_copies()
            casts = cast_steps()
            third = len(casts) // 3
            z = in_proj()
            for step in casts[:third]:
                step()
            gates = gate_proj(z)
            for step in casts[third:2 * third]:
                step()
            x1_buf[...] = mixer_rest(z, *gates)
            for step in casts[2 * third:]:
                step()
            for name in cast_names:
                writeback(name).start()
            finish_weight_copies()

        @pl.when(jnp.logical_and(n > 0, live))
        def _steady_step():
            z = in_proj()
            mlp_state = mlp_begin(x1_buf[...], n_ff - 1)
            gates = gate_proj(z)
            mlp_finish(*mlp_state)
            x1_buf[...] = mixer_rest(z, *gates)

        @pl.when(n == n_tiles)
        def _last_step():
            mlp_finish(*mlp_begin(x1_buf[...], 0))

        if cast_names:
            @pl.when(n == 1)
            def _writebacks_done():
                for name in cast_names:
                    writeback(name).wait()

        pl.when(live)(store_state)
    else:
        if n_tiles == 1:
            start_weight_copies()
        else:
            @pl.when(n == 0)
            def _load_weights():
                start_weight_copies()
                finish_weight_copies()
        z = in_proj()
        x1 = mixer_rest(z, *gate_proj(z))
        if emit_y:
            mlp_finish(*mlp_begin(x1, 0))
        finish_weight_copies()
        store_state()


def _matmul_weights(emit_y):
    return ("w_in", "w_gate") + (("w_out", "w_up", "w_down") if emit_y else ())


def _const_spec(arr):
    nd = arr.ndim
    return pl.BlockSpec(arr.shape, lambda n, _nd=nd: (0,) * _nd, pipeline_mode=pl.Buffered(1))


def _run_layer(x, conv0, h0, dw0, params, *, n_streams, tm, emit_y, final_norm, cast_names=()):
    B, T, D = x.shape
    S, TM = n_streams, tm
    assert B % S == 0 and T % TM == 0 and TM % SUBLANES == 0
    rnn_conv_w, d_rnn = params["cw"].shape
    dw_conv_w, d_conv = params["dww"].shape
    assert d_rnn % MXU_DIM == 0 and d_conv % LANES == 0
    cpad = -(-(rnn_conv_w - 1) // SUBLANES) * SUBLANES
    vpad = -(-(dw_conv_w - 1) // SUBLANES) * SUBLANES
    conv_rows = min(64, TM)
    assert TM % conv_rows == 0 and conv_rows % (2 * SUBLANES) == 0
    d_ff = params["w_up"].shape[-1]
    ff_chunk = min(1024, d_ff)
    tps = T // TM
    n_tiles = (B // S) * tps
    pipelined = emit_y and n_tiles > 1

    def in_tile(n):
        return jnp.minimum(n, n_tiles - 1) if pipelined else n

    def out_tile(n):
        return jnp.maximum(n - 1, 0) if pipelined else n

    def state_spec(arr):
        blk = (S,) + arr.shape[1:]
        if arr.shape[0] == B:
            return pl.BlockSpec(blk, lambda n: (in_tile(n) // tps, 0, 0))
        assert arr.shape[0] == 1 and S == 1
        return pl.BlockSpec(blk, lambda n: (0, 0, 0))

    order = ["norm_mix", "w_in", "cw", "cb", "w_gate", "b_r", "b_i", "lam", "dww", "dwb", "ln_g", "ln_b",
             "onr", "onc", "w_out", "norm_mlp", "w_up", "w_down", "norm_final"]
    weights = [params[k] for k in order]
    in_specs = [pl.BlockSpec((S, TM, D), lambda n: (in_tile(n) // tps, in_tile(n) % tps, 0)),
                state_spec(conv0), state_spec(h0), state_spec(dw0)]
    in_hbm = ("w_in", "w_gate", "w_out", "w_up", "w_down")
    in_specs += [pl.BlockSpec(memory_space=pl.ANY) if k in in_hbm else _const_spec(w) for k, w in zip(order, weights)]

    f32 = jnp.float32
    out_shape = [jax.ShapeDtypeStruct((B, rnn_conv_w - 1, d_rnn), f32),
                 jax.ShapeDtypeStruct((B, 1, d_rnn), f32),
                 jax.ShapeDtypeStruct((B, dw_conv_w - 1, d_conv), f32)]
    out_specs = [pl.BlockSpec((S, rnn_conv_w - 1, d_rnn), lambda n: (in_tile(n) // tps, 0, 0)),
                 pl.BlockSpec((S, 1, d_rnn), lambda n: (in_tile(n) // tps, 0, 0)),
                 pl.BlockSpec((S, dw_conv_w - 1, d_conv), lambda n: (in_tile(n) // tps, 0, 0))]
    if emit_y:
        out_shape = [jax.ShapeDtypeStruct((B, T, D), x.dtype)] + out_shape
        out_specs = [pl.BlockSpec((S, TM, D), lambda n: (out_tile(n) // tps, out_tile(n) % tps, 0))] + out_specs

    M = S * TM
    lvl_rows = _scan_level_rows(TM)
    scratch_shapes = [
        pltpu.VMEM((S, d_rnn // LANES, cpad + TM, LANES), f32),
        pltpu.VMEM((S, d_conv // LANES, vpad + TM, LANES), f32),
        pltpu.VMEM((S, SUBLANES, d_rnn), f32),
        pltpu.VMEM((d_rnn // LANES, M, LANES), f32),
        pltpu.VMEM((d_rnn // LANES, M, LANES), f32),
        pltpu.VMEM((d_rnn // LANES, M, LANES), f32),
        pltpu.VMEM((2, S, d_rnn // LANES, max(sum(lvl_rows), SUBLANES), LANES), f32),
        pltpu.VMEM((S, d_rnn // LANES, max(sum(r + 2 * SUBLANES for r in lvl_rows), SUBLANES), LANES), f32),
        pltpu.VMEM((d_rnn // LANES, M, LANES), f32),
        pltpu.VMEM((d_conv // LANES, M, LANES), f32),
    ]
    w_names = _matmul_weights(emit_y)
    assert all(k in w_names for k in cast_names) and (pipelined or not cast_names)
    scratch_shapes += [pltpu.VMEM(params[k].shape, jnp.bfloat16) for k in w_names]
    scratch_shapes.append(pltpu.SemaphoreType.DMA((len(w_names),)))
    if cast_names:
        for k in cast_names:
            rows = max(CAST_CHUNK_BYTES // (params[k].shape[1] * 4), 2 * SUBLANES)
            assert params[k].shape[0] % rows == 0
            scratch_shapes.append(pltpu.VMEM((2, rows, params[k].shape[1]), f32))
        scratch_shapes += [pltpu.SemaphoreType.DMA((len(cast_names), 2)), pltpu.SemaphoreType.DMA((len(cast_names),))]
        out_shape += [jax.ShapeDtypeStruct(params[k].shape, jnp.bfloat16) for k in cast_names]
        out_specs += [pl.BlockSpec(memory_space=pl.ANY) for _ in cast_names]
    if pipelined:
        scratch_shapes.append(pltpu.VMEM((M, D), f32))
    kern = functools.partial(_layer_kernel, n_streams=S, tm=TM, n_tiles=n_tiles, tiles_per_stream=tps,
                             emit_y=emit_y, final_norm=final_norm, pipelined=pipelined,
                             rnn_conv_w=rnn_conv_w, dw_conv_w=dw_conv_w, ff_chunk=ff_chunk, conv_rows=conv_rows,
                             cast_names=tuple(cast_names))
    return pl.pallas_call(
        kern,
        grid=(n_tiles + (1 if pipelined else 0),),
        in_specs=in_specs,
        out_specs=out_specs,
        out_shape=out_shape,
        scratch_shapes=scratch_shapes,
        compiler_params=pltpu.CompilerParams(
            dimension_semantics=("arbitrary",),
            vmem_limit_bytes=VMEM_LIMIT_BYTES),
        name=f"hybrid_layer_s{S}_t{TM}",
    )(x, conv0, h0, dw0, *weights)


def _block_diag_gates(w_r, w_i):
    n_heads, hd, _ = w_r.shape
    per = MXU_DIM // hd
    n_slab = n_heads // per
    on_diag = jnp.eye(per, dtype=bool)[None, :, None, :, None]

    def slabs(w):
        w = w.reshape(n_slab, per, hd, hd)
        bd = jnp.where(on_diag, w[:, :, :, None, :], 0.0)
        return bd.reshape(n_slab, MXU_DIM, MXU_DIM)

    return jnp.concatenate([slabs(w_r), slabs(w_i)], axis=-1).astype(jnp.bfloat16)


def kernel(x_prompt, x_sample, state_rglru_conv, state_rglru_h, state_dwconv, meta_tokens, norm_mix, w_in,
           rnn_conv_w, rnn_conv_b, w_gate_r, b_gate_r, w_gate_i, b_gate_i, rglru_lambda, dw_w, dw_b, ln_conv_g,
           ln_conv_b, out_norm_rnn, out_norm_conv, w_out, norm_mlp, w_up, w_down, norm_final):
    depth = norm_mix.shape[0]
    bf16 = jnp.bfloat16
    d_rnn = rnn_conv_w.shape[-1]
    d_conv = dw_w.shape[-1]
    c_hist = rnn_conv_w.shape[1] - 1
    v_hist = dw_w.shape[1] - 1

    xm = meta_tokens[None].astype(x_prompt.dtype)
    xp, xs = x_prompt, x_sample
    outs_p, outs_s = [], []
    for l in range(depth):
        last = l == depth - 1
        params = dict(
            norm_mix=norm_mix[l][None], w_in=w_in[l].astype(bf16), cw=rnn_conv_w[l], cb=rnn_conv_b[l][None],
            w_gate=_block_diag_gates(w_gate_r[l], w_gate_i[l]), b_r=b_gate_r[l][None], b_i=b_gate_i[l][None],
            lam=rglru_lambda[l][None], dww=dw_w[l], dwb=dw_b[l][None], ln_g=ln_conv_g[l][None],
            ln_b=ln_conv_b[l][None], onr=out_norm_rnn[l][None], onc=out_norm_conv[l][None],
            w_out=w_out[l].astype(bf16), norm_mlp=norm_mlp[l][None], w_up=w_up[l], w_down=w_down[l],
            norm_final=norm_final[None])
        tm_p = min(512, xp.shape[1])
        prompt_tiles = xp.shape[0] * (xp.shape[1] // tm_p)
        prompt_casts = ("w_up", "w_down") if last and prompt_tiles > 1 else ()
        if not prompt_casts:
            params.update(w_up=w_up[l].astype(bf16), w_down=w_down[l].astype(bf16))
        f32 = jnp.float32
        zero_state = (jnp.zeros((1, c_hist, d_rnn), f32), jnp.zeros((1, 1, d_rnn), f32),
                      jnp.zeros((1, v_hist, d_conv), f32))
        res_m = _run_layer(xm, *zero_state, params, n_streams=1, tm=xm.shape[1], emit_y=not last,
                           final_norm=False)
        if not last:
            xm, res_m = res_m[0], res_m[1:]
        res_p = _run_layer(xp, *res_m, params, n_streams=1, tm=tm_p, emit_y=True, final_norm=last,
                           cast_names=prompt_casts)
        if prompt_casts:
            params.update(zip(prompt_casts, res_p[4:]))
            res_p = res_p[:4]
        res_s = _run_layer(xs, state_rglru_conv[l], state_rglru_h[l][:, None], state_dwconv[l], params,
                           n_streams=xs.shape[0], tm=xs.shape[1], emit_y=True, final_norm=last)
        xp, xs = res_p[0], res_s[0]
        outs_p.append(res_p[1:])
        outs_s.append(res_s[1:])

    def stack(outs, i):
        return outs[0][i][None] if depth == 1 else jnp.stack([o[i] for o in outs])

    return (xp, xs,
            stack(outs_p, 0), stack(outs_p, 1)[:, :, 0], stack(outs_p, 2),
            stack(outs_s, 0), stack(outs_s, 1)[:, :, 0], stack(outs_s, 2))
```

```python
import functools
import math

import jax
import jax.numpy as jnp
from jax import lax
from jax.experimental import pallas as pl
from jax.experimental.pallas import tpu as pltpu

EPS = 1e-6
RGLRU_C = 8.0
SUBLANES = 8
LANES = 128
MXU_DIM = 256
VMEM_LIMIT_BYTES = 60000 * 1024
CAST_CHUNK_BYTES = 1 << 20
CAST_SLOTS = 4


def _rms(x, g):
    ms = jnp.mean(x * x, axis=-1, keepdims=True)
    return x * lax.rsqrt(ms + EPS) * g


def _sigmoid(x):
    return 0.5 * jnp.tanh(0.5 * x) + 0.5


def _silu(x):
    h = 0.5 * x
    return h * (jnp.tanh(h) + 1.0)


def _gelu_tanh(x):
    c = math.sqrt(2.0 / math.pi)
    inner = x * (c + (0.044715 * c) * (x * x))
    return (0.5 * x) * (1.0 + jnp.tanh(inner))


def _bf16_dot(a, b):
    return jnp.dot(a.astype(jnp.bfloat16), b, preferred_element_type=jnp.float32)


def _to_slabs(dst, idx, x):
    for c in range(x.shape[-1] // LANES):
        dst[idx + (c,)] = x[:, c * LANES:(c + 1) * LANES]


def _from_slabs(buf):
    return jnp.concatenate([buf[c] for c in range(buf.shape[0])], axis=-1)


def _strided_causal_conv(win, s, w_ref, b_ref, out, *, taps, lead, tm, rows):
    half = rows // 2
    for c in range(win.shape[1]):
        lanes = slice(c * LANES, (c + 1) * LANES)
        for t0 in [i * rows + par for i in range(tm // rows) for par in range(2)]:
            acc = win[s, c, pl.ds(lead + t0, half, stride=2), :] * w_ref[0:1, lanes]
            for k in range(1, taps):
                acc = acc + win[s, c, pl.ds(lead + t0 + k, half, stride=2), :] * w_ref[k:k + 1, lanes]
            out[c, pl.ds(s * tm + t0, half, stride=2), :] = acc + b_ref[:, lanes]


def _scan_radices(n_rows):
    n_groups, radices = n_rows // SUBLANES, []
    while n_groups > 1:
        assert n_groups % 2 == 0
        r = 4 if n_groups % 4 == 0 else 2
        radices.append(r)
        n_groups //= r
    return radices


def _scan_level_rows(n_rows):
    rows, out = n_rows, []
    for r in _scan_radices(n_rows)[:-1]:
        rows //= r
        out.append(rows)
    return out


def _linear_scan_slab(a_ref, b_ref, h_ref, ab_lvl, hin_lvl, base, n_rows, carry):
    radices = _scan_radices(n_rows)
    lvl_rows = _scan_level_rows(n_rows)
    ab_off = [sum(lvl_rows[:i]) for i in range(len(lvl_rows))]
    hin_off = [sum(r + 2 * SUBLANES for r in lvl_rows[:i]) for i in range(len(lvl_rows))]

    partial = []
    rows = n_rows
    for lvl, r in enumerate(radices):
        g = rows // r
        src, off = ((a_ref, b_ref), base) if lvl == 0 else ((ab_lvl.at[0], ab_lvl.at[1]), ab_off[lvl - 1])
        ld = lambda kind, k: src[kind][pl.ds(off + k, g, stride=r), :]
        pa, pb = ld(0, 0), ld(1, 0)
        maps = [(pa, pb)]
        for k in range(1, r):
            ak, bk = ld(0, k), ld(1, k)
            pa, pb = ak * pa, ak * pb + bk
            maps.append((pa, pb))
        partial.append(maps)
        rows = g
        if lvl + 1 < len(radices):
            ab_lvl[0, ab_off[lvl]:ab_off[lvl] + g, :] = pa
            ab_lvl[1, ab_off[lvl]:ab_off[lvl] + g, :] = pb
    if not radices:
        pa, pb = a_ref[base:base + SUBLANES, :], b_ref[base:base + SUBLANES, :]

    row = lax.broadcasted_iota(jnp.int32, (SUBLANES, LANES), 0)
    for sh in (1, 2, 4):
        keep = row >= sh
        pa_s = jnp.where(keep, pltpu.roll(pa, sh, 0), 1.0)
        pb_s = jnp.where(keep, pltpu.roll(pb, sh, 0), 0.0)
        pa, pb = pa * pa_s, pa * pb_s + pb
    h_top = pa * carry + pb
    new_carry = jnp.broadcast_to(h_top[SUBLANES - 1:SUBLANES, :], (SUBLANES, LANES))
    if not radices:
        h_ref[base:base + SUBLANES, :] = h_top
        return new_carry
    h_prev = jnp.where(row == 0, carry, pltpu.roll(h_top, 1, 0))

    for lvl in reversed(range(len(radices))):
        r = radices[lvl]
        g = partial[lvl][0][0].shape[0]
        if lvl < len(radices) - 1:
            o = hin_off[lvl] + SUBLANES
            h_prev = hin_lvl[o:o + g, :]
        for k, (pa, pb) in enumerate(partial[lvl]):
            hk = pa * h_prev + pb
            if lvl == 0:
                h_ref[pl.ds(base + k, g, stride=r), :] = hk
            else:
                o = hin_off[lvl - 1] + SUBLANES
                hin_lvl[pl.ds(o + 1 + k, g, stride=r), :] = hk
        if lvl > 0:
            o = hin_off[lvl - 1] + SUBLANES
            hin_lvl[o:o + 1, :] = carry[0:1, :]
    return new_carry


def _layer_kernel(x_ref, conv0_ref, h0_ref, dw0_ref,
                  norm_mix_ref, w_in_ref, cw_ref, cb_ref, w_gate_ref, b_r_ref, b_i_ref, lam_ref,
                  dww_ref, dwb_ref, ln_g_ref, ln_b_ref, onr_ref, onc_ref,
                  w_out_ref, norm_mlp_ref, w_up_ref, w_down_ref, norm_final_ref,
                  *rest, n_streams, tm, n_tiles, tiles_per_stream, emit_y, final_norm, pipelined,
                  rnn_conv_w, dw_conv_w, ff_chunk, conv_rows, cast_names):
    n_out = 4 if emit_y else 3
    if emit_y:
        y_ref, conv_out_ref, h_out_ref, dw_out_ref = rest[:4]
    else:
        conv_out_ref, h_out_ref, dw_out_ref = rest[:3]
    w_bf16_out = dict(zip(cast_names, rest[n_out:n_out + len(cast_names)]))
    scratch = rest[n_out + len(cast_names):]
    cbuf, vbuf, hcar, a_buf, b_buf, h_buf, ab_lvl, hin_lvl, xc_buf, vc_buf = scratch[:10]

    w_hbm = dict(w_in=w_in_ref, w_gate=w_gate_ref, w_out=w_out_ref, w_up=w_up_ref, w_down=w_down_ref)
    w_names = _matmul_weights(emit_y)
    w_vmem = dict(zip(w_names, scratch[10:10 + len(w_names)]))
    w_sem = scratch[10 + len(w_names)]
    scratch = scratch[:10] + scratch[11 + len(w_names):]
    stage = dict(zip(cast_names, scratch[10:10 + len(cast_names)]))
    if cast_names:
        stage_sem, writeback_sem = scratch[10 + len(cast_names):12 + len(cast_names)]
        scratch = scratch[:10] + scratch[12 + len(cast_names):]
    in_flight = set()

    def weight_copy(name):
        return pltpu.make_async_copy(w_hbm[name], w_vmem[name], w_sem.at[w_names.index(name)])

    def start_weight_copies():
        for name in w_names:
            if name not in cast_names:
                weight_copy(name).start()
                in_flight.add(name)

    def cast_steps():
        items = [(name, c) for name in cast_names for c in range(w_hbm[name].shape[0] // stage[name].shape[1])]

        ahead = CAST_SLOTS - 1

        def fetch(name, c):
            rows = stage[name].shape[1]
            return pltpu.make_async_copy(w_hbm[name].at[pl.ds(c * rows, rows), :], stage[name].at[c % CAST_SLOTS],
                                         stage_sem.at[cast_names.index(name), c % CAST_SLOTS])

        def step(i):
            name, c = items[i]
            for j in (range(min(ahead, len(items))) if i == 0 else ()):
                fetch(*items[j]).start()
            if i + ahead < len(items):
                fetch(*items[i + ahead]).start()
            fetch(name, c).wait()
            rows = stage[name].shape[1]
            w_vmem[name][c * rows:(c + 1) * rows, :] = stage[name][c % CAST_SLOTS].astype(jnp.bfloat16)

        return [functools.partial(step, i) for i in range(len(items))]

    def writeback(name):
        return pltpu.make_async_copy(w_vmem[name], w_bf16_out[name], writeback_sem.at[cast_names.index(name)])

    def finish_weight_copies():
        for name in w_names:
            weight(name)

    def weight(name):
        if name in in_flight:
            weight_copy(name).wait()
            in_flight.discard(name)
        return w_vmem[name]

    S, TM = n_streams, tm
    M = S * TM
    d_rnn = cw_ref.shape[-1]
    d_conv = dww_ref.shape[-1]
    cpad = cbuf.shape[2] - TM
    vpad = vbuf.shape[2] - TM
    c_hist = rnn_conv_w - 1
    v_hist = dw_conv_w - 1
    n = pl.program_id(0)
    tile = jnp.minimum(n, n_tiles - 1)
    live = n < n_tiles

    @pl.when(jnp.logical_and(lax.rem(tile, tiles_per_stream) == 0, live))
    def _load_state():
        cbuf[:, :, 0:cpad - c_hist, :] = jnp.zeros((S, d_rnn // LANES, cpad - c_hist, LANES), jnp.float32)
        vbuf[:, :, 0:vpad - v_hist, :] = jnp.zeros((S, d_conv // LANES, vpad - v_hist, LANES), jnp.float32)
        for c in range(d_rnn // LANES):
            cbuf[:, c, cpad - c_hist:cpad, :] = conv0_ref[:, :, c * LANES:(c + 1) * LANES]
        for c in range(d_conv // LANES):
            vbuf[:, c, vpad - v_hist:vpad, :] = dw0_ref[:, :, c * LANES:(c + 1) * LANES]
        hcar[...] = jnp.broadcast_to(h0_ref[...], hcar.shape)

    def in_proj():
        x = x_ref[...].reshape(M, x_ref.shape[-1])
        hn = _rms(x, norm_mix_ref[...])
        return _bf16_dot(hn, weight("w_in")[...])

    def gate_proj(z):
        xr = z[:, 0:d_rnn]
        for s in range(S):
            _to_slabs(cbuf.at[:, :, cpad:cpad + TM, :], (s,), xr[s * TM:(s + 1) * TM, :])
            _strided_causal_conv(cbuf, s, cw_ref, cb_ref, xc_buf, taps=rnn_conv_w, lead=cpad - c_hist, tm=TM,
                                 rows=conv_rows)
            for c in range(d_rnn // LANES):
                cbuf[s, c, 0:cpad, :] = cbuf[s, c, TM:TM + cpad, :]
        xr_c = _from_slabs(xc_buf)

        xb = xr_c.astype(jnp.bfloat16)
        r_parts, i_parts = [], []
        for j in range(d_rnn // MXU_DIM):
            ri = jnp.dot(xb[:, j * MXU_DIM:(j + 1) * MXU_DIM], weight("w_gate")[j],
                         preferred_element_type=jnp.float32)
            r_parts.append(ri[:, :MXU_DIM])
            i_parts.append(ri[:, MXU_DIM:])
        return xr_c, r_parts, i_parts

    def mixer_rest(z, xr_c, r_parts, i_parts):
        x = x_ref[...].reshape(M, x_ref.shape[-1])
        gate = z[:, d_rnn:2 * d_rnn]
        glu_v = z[:, 2 * d_rnn:2 * d_rnn + d_conv]
        glu_g = z[:, 2 * d_rnn + d_conv:]
        r = _sigmoid(jnp.concatenate(r_parts, axis=-1) + b_r_ref[...])
        ig = _sigmoid(jnp.concatenate(i_parts, axis=-1) + b_i_ref[...])
        nl = -lam_ref[...]
        softplus_nl = jnp.maximum(nl, 0.0) + jnp.log1p(jnp.exp(-jnp.abs(nl)))
        log_a = (-RGLRU_C * r) * softplus_nl
        a = jnp.exp(log_a)
        mult = jnp.sqrt(jnp.maximum(1.0 - a * a, 0.0))
        _to_slabs(a_buf, (), a)
        _to_slabs(b_buf, (), mult * ig * xr_c)
        for s in range(S):
            carry = hcar[s]
            hcar[s] = jnp.concatenate(
                [_linear_scan_slab(a_buf.at[c], b_buf.at[c], h_buf.at[c], ab_lvl.at[:, s, c], hin_lvl.at[s, c],
                                   s * TM, TM, carry[:, c * LANES:(c + 1) * LANES])
                 for c in range(d_rnn // LANES)], axis=-1)

        v = glu_v * _sigmoid(glu_g)
        for s in range(S):
            _to_slabs(vbuf.at[:, :, vpad:vpad + TM, :], (s,), v[s * TM:(s + 1) * TM, :])
            _strided_causal_conv(vbuf, s, dww_ref, dwb_ref, vc_buf, taps=dw_conv_w, lead=vpad - v_hist, tm=TM,
                                 rows=conv_rows)
            for c in range(d_conv // LANES):
                vbuf[s, c, 0:vpad, :] = vbuf[s, c, TM:TM + vpad, :]
        if not emit_y:
            return None

        y_rnn = _from_slabs(h_buf) * _gelu_tanh(gate)
        vc = _from_slabs(vc_buf)
        mu = jnp.mean(vc, axis=-1, keepdims=True)
        vcc = vc - mu
        ln = vcc * lax.rsqrt(jnp.mean(vcc * vcc, axis=-1, keepdims=True) + EPS) * ln_g_ref[...] + ln_b_ref[...]
        y_conv = _silu(ln)
        mix = jnp.concatenate([_rms(y_rnn, onr_ref[...]), _rms(y_conv, onc_ref[...])], axis=-1)
        return x + _bf16_dot(mix, weight("w_out")[...])

    n_ff = w_up_ref.shape[-1] // ff_chunk

    def mlp_chunk(hm, x2, c):
        hc = jnp.dot(hm, weight("w_up")[:, c * ff_chunk:(c + 1) * ff_chunk], preferred_element_type=jnp.float32)
        hc = jnp.square(jnp.maximum(hc, 0.0))
        return x2 + _bf16_dot(hc, weight("w_down")[c * ff_chunk:(c + 1) * ff_chunk, :])

    def mlp_begin(x1, n_chunks):
        hm = _rms(x1, norm_mlp_ref[...]).astype(jnp.bfloat16)
        x2 = x1
        for c in range(n_chunks):
            x2 = mlp_chunk(hm, x2, c)
        return hm, x2, n_chunks

    def mlp_finish(hm, x2, n_done):
        for c in range(n_done, n_ff):
            x2 = mlp_chunk(hm, x2, c)
        if final_norm:
            x2 = _rms(x2, norm_final_ref[...])
        y_ref[...] = x2.reshape(y_ref.shape)

    def store_state():
        for s in range(S):
            for c in range(d_rnn // LANES):
                conv_out_ref[s, :, c * LANES:(c + 1) * LANES] = cbuf[s, c, cpad - c_hist:cpad, :]
            for c in range(d_conv // LANES):
                dw_out_ref[s, :, c * LANES:(c + 1) * LANES] = vbuf[s, c, vpad - v_hist:vpad, :]
            h_out_ref[s] = hcar[s, 0:1, :]

    if pipelined:
        x1_buf = scratch[10]

        @pl.when(n == 0)
        def _first_step():
            start_weight_copies()
            casts = cast_steps()
            third = len(casts) // 3
            z = in_proj()
            for step in casts[:third]:
                step()
            gates = gate_proj(z)
            for step in casts[third:2 * third]:
                step()
            x1_buf[...] = mixer_rest(z, *gates)
            for step in casts[2 * third:]:
                step()
            for name in cast_names:
                writeback(name).start()
            finish_weight_copies()

        @pl.when(jnp.logical_and(n > 0, live))
        def _steady_step():
            z = in_proj()
            mlp_state = mlp_begin(x1_buf[...], n_ff - 1)
            gates = gate_proj(z)
            mlp_finish(*mlp_state)
            x1_buf[...] = mixer_rest(z, *gates)

        @pl.when(n == n_tiles)
        def _last_step():
            mlp_finish(*mlp_begin(x1_buf[...], 0))

        if cast_names:
            @pl.when(n == 1)
            def _writebacks_done():
                for name in cast_names:
                    writeback(name).wait()

        pl.when(live)(store_state)
    else:
        if n_tiles == 1:
            start_weight_copies()
        else:
            @pl.when(n == 0)
            def _load_weights():
                start_weight_copies()
                finish_weight_copies()
        z = in_proj()
        x1 = mixer_rest(z, *gate_proj(z))
        if emit_y:
            mlp_finish(*mlp_begin(x1, 0))
        finish_weight_copies()
        store_state()


def _matmul_weights(emit_y):
    return ("w_in", "w_gate") + (("w_out", "w_up", "w_down") if emit_y else ())


def _const_spec(arr):
    nd = arr.ndim
    return pl.BlockSpec(arr.shape, lambda n, _nd=nd: (0,) * _nd, pipeline_mode=pl.Buffered(1))


def _run_layer(x, conv0, h0, dw0, params, *, n_streams, tm, emit_y, final_norm, cast_names=()):
    B, T, D = x.shape
    S, TM = n_streams, tm
    assert B % S == 0 and T % TM == 0 and TM % SUBLANES == 0
    rnn_conv_w, d_rnn = params["cw"].shape
    dw_conv_w, d_conv = params["dww"].shape
    assert d_rnn % MXU_DIM == 0 and d_conv % LANES == 0
    cpad = -(-(rnn_conv_w - 1) // SUBLANES) * SUBLANES
    vpad = -(-(dw_conv_w - 1) // SUBLANES) * SUBLANES
    conv_rows = min(64, TM)
    assert TM % conv_rows == 0 and conv_rows % (2 * SUBLANES) == 0
    d_ff = params["w_up"].shape[-1]
    ff_chunk = min(1024, d_ff)
    tps = T // TM
    n_tiles = (B // S) * tps
    pipelined = emit_y and n_tiles > 1

    def in_tile(n):
        return jnp.minimum(n, n_tiles - 1) if pipelined else n

    def out_tile(n):
        return jnp.maximum(n - 1, 0) if pipelined else n

    def state_spec(arr):
        blk = (S,) + arr.shape[1:]
        if arr.shape[0] == B:
            return pl.BlockSpec(blk, lambda n: (in_tile(n) // tps, 0, 0))
        assert arr.shape[0] == 1 and S == 1
        return pl.BlockSpec(blk, lambda n: (0, 0, 0))

    order = ["norm_mix", "w_in", "cw", "cb", "w_gate", "b_r", "b_i", "lam", "dww", "dwb", "ln_g", "ln_b",
             "onr", "onc", "w_out", "norm_mlp", "w_up", "w_down", "norm_final"]
    weights = [params[k] for k in order]
    in_specs = [pl.BlockSpec((S, TM, D), lambda n: (in_tile(n) // tps, in_tile(n) % tps, 0)),
                state_spec(conv0), state_spec(h0), state_spec(dw0)]
    in_hbm = ("w_in", "w_gate", "w_out", "w_up", "w_down")
    in_specs += [pl.BlockSpec(memory_space=pl.ANY) if k in in_hbm else _const_spec(w) for k, w in zip(order, weights)]

    f32 = jnp.float32
    out_shape = [jax.ShapeDtypeStruct((B, rnn_conv_w - 1, d_rnn), f32),
                 jax.ShapeDtypeStruct((B, 1, d_rnn), f32),
                 jax.ShapeDtypeStruct((B, dw_conv_w - 1, d_conv), f32)]
    out_specs = [pl.BlockSpec((S, rnn_conv_w - 1, d_rnn), lambda n: (in_tile(n) // tps, 0, 0)),
                 pl.BlockSpec((S, 1, d_rnn), lambda n: (in_tile(n) // tps, 0, 0)),
                 pl.BlockSpec((S, dw_conv_w - 1, d_conv), lambda n: (in_tile(n) // tps, 0, 0))]
    if emit_y:
        out_shape = [jax.ShapeDtypeStruct((B, T, D), x.dtype)] + out_shape
        out_specs = [pl.BlockSpec((S, TM, D), lambda n: (out_tile(n) // tps, out_tile(n) % tps, 0))] + out_specs

    M = S * TM
    lvl_rows = _scan_level_rows(TM)
    scratch_shapes = [
        pltpu.VMEM((S, d_rnn // LANES, cpad + TM, LANES), f32),
        pltpu.VMEM((S, d_conv // LANES, vpad + TM, LANES), f32),
        pltpu.VMEM((S, SUBLANES, d_rnn), f32),
        pltpu.VMEM((d_rnn // LANES, M, LANES), f32),
        pltpu.VMEM((d_rnn // LANES, M, LANES), f32),
        pltpu.VMEM((d_rnn // LANES, M, LANES), f32),
        pltpu.VMEM((2, S, d_rnn // LANES, max(sum(lvl_rows), SUBLANES), LANES), f32),
        pltpu.VMEM((S, d_rnn // LANES, max(sum(r + 2 * SUBLANES for r in lvl_rows), SUBLANES), LANES), f32),
        pltpu.VMEM((d_rnn // LANES, M, LANES), f32),
        pltpu.VMEM((d_conv // LANES, M, LANES), f32),
    ]
    w_names = _matmul_weights(emit_y)
    assert all(k in w_names for k in cast_names) and (pipelined or not cast_names)
    scratch_shapes += [pltpu.VMEM(params[k].shape, jnp.bfloat16) for k in w_names]
    scratch_shapes.append(pltpu.SemaphoreType.DMA((len(w_names),)))
    if cast_names:
        for k in cast_names:
            rows = max(CAST_CHUNK_BYTES // (params[k].shape[1] * 4), 2 * SUBLANES)
            assert params[k].shape[0] % rows == 0
            scratch_shapes.append(pltpu.VMEM((CAST_SLOTS, rows, params[k].shape[1]), f32))
        scratch_shapes += [pltpu.SemaphoreType.DMA((len(cast_names), CAST_SLOTS)),
                           pltpu.SemaphoreType.DMA((len(cast_names),))]
        out_shape += [jax.ShapeDtypeStruct(params[k].shape, jnp.bfloat16) for k in cast_names]
        out_specs += [pl.BlockSpec(memory_space=pl.ANY) for _ in cast_names]
    if pipelined:
        scratch_shapes.append(pltpu.VMEM((M, D), f32))
    kern = functools.partial(_layer_kernel, n_streams=S, tm=TM, n_tiles=n_tiles, tiles_per_stream=tps,
                             emit_y=emit_y, final_norm=final_norm, pipelined=pipelined,
                             rnn_conv_w=rnn_conv_w, dw_conv_w=dw_conv_w, ff_chunk=ff_chunk, conv_rows=conv_rows,
                             cast_names=tuple(cast_names))
    return pl.pallas_call(
        kern,
        grid=(n_tiles + (1 if pipelined else 0),),
        in_specs=in_specs,
        out_specs=out_specs,
        out_shape=out_shape,
        scratch_shapes=scratch_shapes,
        compiler_params=pltpu.CompilerParams(
            dimension_semantics=("arbitrary",),
            vmem_limit_bytes=VMEM_LIMIT_BYTES),
        name=f"hybrid_layer_s{S}_t{TM}",
    )(x, conv0, h0, dw0, *weights)


def _block_diag_gates(w_r, w_i):
    n_heads, hd, _ = w_r.shape
    per = MXU_DIM // hd
    n_slab = n_heads // per
    on_diag = jnp.eye(per, dtype=bool)[None, :, None, :, None]

    def slabs(w):
        w = w.reshape(n_slab, per, hd, hd)
        bd = jnp.where(on_diag, w[:, :, :, None, :], 0.0)
        return bd.reshape(n_slab, MXU_DIM, MXU_DIM)

    return jnp.concatenate([slabs(w_r), slabs(w_i)], axis=-1).astype(jnp.bfloat16)


def kernel(x_prompt, x_sample, state_rglru_conv, state_rglru_h, state_dwconv, meta_tokens, norm_mix, w_in,
           rnn_conv_w, rnn_conv_b, w_gate_r, b_gate_r, w_gate_i, b_gate_i, rglru_lambda, dw_w, dw_b, ln_conv_g,
           ln_conv_b, out_norm_rnn, out_norm_conv, w_out, norm_mlp, w_up, w_down, norm_final):
    depth = norm_mix.shape[0]
    bf16 = jnp.bfloat16
    d_rnn = rnn_conv_w.shape[-1]
    d_conv = dw_w.shape[-1]
    c_hist = rnn_conv_w.shape[1] - 1
    v_hist = dw_w.shape[1] - 1

    xm = meta_tokens[None].astype(x_prompt.dtype)
    xp, xs = x_prompt, x_sample
    outs_p, outs_s = [], []
    for l in range(depth):
        last = l == depth - 1
        params = dict(
            norm_mix=norm_mix[l][None], w_in=w_in[l].astype(bf16), cw=rnn_conv_w[l], cb=rnn_conv_b[l][None],
            w_gate=_block_diag_gates(w_gate_r[l], w_gate_i[l]), b_r=b_gate_r[l][None], b_i=b_gate_i[l][None],
            lam=rglru_lambda[l][None], dww=dw_w[l], dwb=dw_b[l][None], ln_g=ln_conv_g[l][None],
            ln_b=ln_conv_b[l][None], onr=out_norm_rnn[l][None], onc=out_norm_conv[l][None],
            w_out=w_out[l].astype(bf16), norm_mlp=norm_mlp[l][None], w_up=w_up[l], w_down=w_down[l],
            norm_final=norm_final[None])
        tm_p = min(512, xp.shape[1])
        prompt_tiles = xp.shape[0] * (xp.shape[1] // tm_p)
        prompt_casts = ("w_up", "w_down") if last and prompt_tiles > 1 else ()
        if not prompt_casts:
            params.update(w_up=w_up[l].astype(bf16), w_down=w_down[l].astype(bf16))
        f32 = jnp.float32
        zero_state = (jnp.zeros((1, c_hist, d_rnn), f32), jnp.zeros((1, 1, d_rnn), f32),
                      jnp.zeros((1, v_hist, d_conv), f32))
        res_m = _run_layer(xm, *zero_state, params, n_streams=1, tm=xm.shape[1], emit_y=not last,
                           final_norm=False)
        if not last:
            xm, res_m = res_m[0], res_m[1:]
        res_p = _run_layer(xp, *res_m, params, n_streams=1, tm=tm_p, emit_y=True, final_norm=last,
                           cast_names=prompt_casts)
        if prompt_casts:
            params.update(zip(prompt_casts, res_p[4:]))
            res_p = res_p[:4]
        res_s = _run_layer(xs, state_rglru_conv[l], state_rglru_h[l][:, None], state_dwconv[l], params,
                           n_streams=xs.shape[0], tm=xs.shape[1], emit_y=True, final_norm=last)
        xp, xs = res_p[0], res_s[0]
        outs_p.append(res_p[1:])
        outs_s.append(res_s[1:])

    def stack(outs, i):
        return outs[0][i][None] if depth == 1 else jnp.stack([o[i] for o in outs])

    return (xp, xs,
            stack(outs_p, 0), stack(outs_p, 1)[:, :, 0], stack(outs_p, 2),
            stack(outs_s, 0), stack(outs_s, 1)[:, :, 0], stack(outs_s, 2))
```

```python
import functools
import math

import jax
import jax.numpy as jnp
from jax import lax
from jax.experimental import pallas as pl
from jax.experimental.pallas import tpu as pltpu

EPS = 1e-6
RGLRU_C = 8.0
SUBLANES = 8
LANES = 128
MXU_DIM = 256
VMEM_LIMIT_BYTES = 60000 * 1024


def _rms(x, g):
    ms = jnp.mean(x * x, axis=-1, keepdims=True)
    return x * lax.rsqrt(ms + EPS) * g


def _sigmoid(x):
    return 0.5 * jnp.tanh(0.5 * x) + 0.5


def _silu(x):
    h = 0.5 * x
    return h * (jnp.tanh(h) + 1.0)


def _gelu_tanh(x):
    c = math.sqrt(2.0 / math.pi)
    inner = x * (c + (0.044715 * c) * (x * x))
    return (0.5 * x) * (1.0 + jnp.tanh(inner))


def _bf16_dot(a, b):
    return jnp.dot(a.astype(jnp.bfloat16), b, preferred_element_type=jnp.float32)


def _to_slabs(dst, idx, x):
    for c in range(x.shape[-1] // LANES):
        dst[idx + (c,)] = x[:, c * LANES:(c + 1) * LANES]


def _from_slabs(buf):
    return jnp.concatenate([buf[c] for c in range(buf.shape[0])], axis=-1)


def _strided_causal_conv(win, s, w_ref, b_ref, out, *, taps, lead, tm, rows):
    half = rows // 2
    for c in range(win.shape[1]):
        lanes = slice(c * LANES, (c + 1) * LANES)
        for t0 in [i * rows + par for i in range(tm // rows) for par in range(2)]:
            acc = win[s, c, pl.ds(lead + t0, half, stride=2), :] * w_ref[0:1, lanes]
            for k in range(1, taps):
                acc = acc + win[s, c, pl.ds(lead + t0 + k, half, stride=2), :] * w_ref[k:k + 1, lanes]
            out[c, pl.ds(s * tm + t0, half, stride=2), :] = acc + b_ref[:, lanes]


def _scan_radices(n_rows):
    n_groups, radices = n_rows // SUBLANES, []
    while n_groups > 1:
        assert n_groups % 2 == 0
        r = 4 if n_groups % 4 == 0 else 2
        radices.append(r)
        n_groups //= r
    return radices


def _scan_level_rows(n_rows):
    rows, out = n_rows, []
    for r in _scan_radices(n_rows)[:-1]:
        rows //= r
        out.append(rows)
    return out


def _linear_scan_slab(a_ref, b_ref, h_ref, ab_lvl, hin_lvl, base, n_rows, carry):
    radices = _scan_radices(n_rows)
    lvl_rows = _scan_level_rows(n_rows)
    ab_off = [sum(lvl_rows[:i]) for i in range(len(lvl_rows))]
    hin_off = [sum(r + 2 * SUBLANES for r in lvl_rows[:i]) for i in range(len(lvl_rows))]

    partial = []
    rows = n_rows
    for lvl, r in enumerate(radices):
        g = rows // r
        src, off = ((a_ref, b_ref), base) if lvl == 0 else ((ab_lvl.at[0], ab_lvl.at[1]), ab_off[lvl - 1])
        ld = lambda kind, k: src[kind][pl.ds(off + k, g, stride=r), :]
        pa, pb = ld(0, 0), ld(1, 0)
        maps = [(pa, pb)]
        for k in range(1, r):
            ak, bk = ld(0, k), ld(1, k)
            pa, pb = ak * pa, ak * pb + bk
            maps.append((pa, pb))
        partial.append(maps)
        rows = g
        if lvl + 1 < len(radices):
            ab_lvl[0, ab_off[lvl]:ab_off[lvl] + g, :] = pa
            ab_lvl[1, ab_off[lvl]:ab_off[lvl] + g, :] = pb
    if not radices:
        pa, pb = a_ref[base:base + SUBLANES, :], b_ref[base:base + SUBLANES, :]

    row = lax.broadcasted_iota(jnp.int32, (SUBLANES, LANES), 0)
    for sh in (1, 2, 4):
        keep = row >= sh
        pa_s = jnp.where(keep, pltpu.roll(pa, sh, 0), 1.0)
        pb_s = jnp.where(keep, pltpu.roll(pb, sh, 0), 0.0)
        pa, pb = pa * pa_s, pa * pb_s + pb
    h_top = pa * carry + pb
    new_carry = jnp.broadcast_to(h_top[SUBLANES - 1:SUBLANES, :], (SUBLANES, LANES))
    if not radices:
        h_ref[base:base + SUBLANES, :] = h_top
        return new_carry
    h_prev = jnp.where(row == 0, carry, pltpu.roll(h_top, 1, 0))

    for lvl in reversed(range(len(radices))):
        r = radices[lvl]
        g = partial[lvl][0][0].shape[0]
        if lvl < len(radices) - 1:
            o = hin_off[lvl] + SUBLANES
            h_prev = hin_lvl[o:o + g, :]
        for k, (pa, pb) in enumerate(partial[lvl]):
            hk = pa * h_prev + pb
            if lvl == 0:
                h_ref[pl.ds(base + k, g, stride=r), :] = hk
            else:
                o = hin_off[lvl - 1] + SUBLANES
                hin_lvl[pl.ds(o + 1 + k, g, stride=r), :] = hk
        if lvl > 0:
            o = hin_off[lvl - 1] + SUBLANES
            hin_lvl[o:o + 1, :] = carry[0:1, :]
    return new_carry


def _layer_kernel(x_ref, conv0_ref, h0_ref, dw0_ref,
                  norm_mix_ref, w_in_ref, cw_ref, cb_ref, w_gate_ref, b_r_ref, b_i_ref, lam_ref,
                  dww_ref, dwb_ref, ln_g_ref, ln_b_ref, onr_ref, onc_ref,
                  w_out_ref, norm_mlp_ref, w_up_ref, w_down_ref, norm_final_ref,
                  *rest, n_streams, tm, n_tiles, tiles_per_stream, emit_y, final_norm, pipelined,
                  rnn_conv_w, dw_conv_w, ff_chunk, conv_rows):
    if emit_y:
        y_ref, conv_out_ref, h_out_ref, dw_out_ref = rest[:4]
        scratch = rest[4:]
    else:
        conv_out_ref, h_out_ref, dw_out_ref = rest[:3]
        scratch = rest[3:]
    cbuf, vbuf, hcar, a_buf, b_buf, h_buf, ab_lvl, hin_lvl, xc_buf, vc_buf = scratch[:10]

    w_hbm = dict(w_in=w_in_ref, w_gate=w_gate_ref, w_out=w_out_ref, w_up=w_up_ref, w_down=w_down_ref)
    w_names = _matmul_weights(emit_y)
    w_vmem = dict(zip(w_names, scratch[10:10 + len(w_names)]))
    w_sem = scratch[10 + len(w_names)]
    scratch = scratch[:10] + scratch[11 + len(w_names):]
    in_flight = set()

    def weight_copy(name):
        return pltpu.make_async_copy(w_hbm[name], w_vmem[name], w_sem.at[w_names.index(name)])

    def start_weight_copies():
        for name in w_names:
            weight_copy(name).start()
            in_flight.add(name)

    def finish_weight_copies():
        for name in w_names:
            weight(name)

    def weight(name):
        if name in in_flight:
            weight_copy(name).wait()
            in_flight.discard(name)
        return w_vmem[name]

    S, TM = n_streams, tm
    M = S * TM
    d_rnn = cw_ref.shape[-1]
    d_conv = dww_ref.shape[-1]
    cpad = cbuf.shape[2] - TM
    vpad = vbuf.shape[2] - TM
    c_hist = rnn_conv_w - 1
    v_hist = dw_conv_w - 1
    n = pl.program_id(0)
    tile = jnp.minimum(n, n_tiles - 1)
    live = n < n_tiles

    @pl.when(jnp.logical_and(lax.rem(tile, tiles_per_stream) == 0, live))
    def _load_state():
        cbuf[:, :, 0:cpad - c_hist, :] = jnp.zeros((S, d_rnn // LANES, cpad - c_hist, LANES), jnp.float32)
        vbuf[:, :, 0:vpad - v_hist, :] = jnp.zeros((S, d_conv // LANES, vpad - v_hist, LANES), jnp.float32)
        for c in range(d_rnn // LANES):
            cbuf[:, c, cpad - c_hist:cpad, :] = conv0_ref[:, :, c * LANES:(c + 1) * LANES]
        for c in range(d_conv // LANES):
            vbuf[:, c, vpad - v_hist:vpad, :] = dw0_ref[:, :, c * LANES:(c + 1) * LANES]
        hcar[...] = jnp.broadcast_to(h0_ref[...], hcar.shape)

    def in_proj():
        x = x_ref[...].reshape(M, x_ref.shape[-1])
        hn = _rms(x, norm_mix_ref[...])
        return _bf16_dot(hn, weight("w_in")[...])

    n_gate_slabs = d_rnn // MXU_DIM
    lanes_per_gate_slab = MXU_DIM // LANES

    def short_conv(z):
        xr = z[:, 0:d_rnn]
        for s in range(S):
            _to_slabs(cbuf.at[:, :, cpad:cpad + TM, :], (s,), xr[s * TM:(s + 1) * TM, :])
            _strided_causal_conv(cbuf, s, cw_ref, cb_ref, xc_buf, taps=rnn_conv_w, lead=cpad - c_hist, tm=TM,
                                 rows=conv_rows)
            for c in range(d_rnn // LANES):
                cbuf[s, c, 0:cpad, :] = cbuf[s, c, TM:TM + cpad, :]

    def gate_proj(j):
        slabs = range(j * lanes_per_gate_slab, (j + 1) * lanes_per_gate_slab)
        xc = jnp.concatenate([xc_buf[c] for c in slabs], axis=-1)
        return xc, _bf16_dot(xc, weight("w_gate")[j])

    def rglru(j, xc, ri):
        lanes = slice(j * MXU_DIM, (j + 1) * MXU_DIM)
        r = _sigmoid(ri[:, :MXU_DIM] + b_r_ref[:, lanes])
        ig = _sigmoid(ri[:, MXU_DIM:] + b_i_ref[:, lanes])
        nl = -lam_ref[:, lanes]
        softplus_nl = jnp.maximum(nl, 0.0) + jnp.log1p(jnp.exp(-jnp.abs(nl)))
        a = jnp.exp((-RGLRU_C * r) * softplus_nl)
        b = jnp.sqrt(jnp.maximum(1.0 - a * a, 0.0)) * ig * xc
        for k in range(lanes_per_gate_slab):
            c = j * lanes_per_gate_slab + k
            a_buf[c] = a[:, k * LANES:(k + 1) * LANES]
            b_buf[c] = b[:, k * LANES:(k + 1) * LANES]
            for s in range(S):
                hcar[s, :, c * LANES:(c + 1) * LANES] = _linear_scan_slab(
                    a_buf.at[c], b_buf.at[c], h_buf.at[c], ab_lvl.at[:, s, c], hin_lvl.at[s, c], s * TM, TM,
                    hcar[s, :, c * LANES:(c + 1) * LANES])

    def mixer_rest(z):
        x = x_ref[...].reshape(M, x_ref.shape[-1])
        gate = z[:, d_rnn:2 * d_rnn]
        glu_v = z[:, 2 * d_rnn:2 * d_rnn + d_conv]
        glu_g = z[:, 2 * d_rnn + d_conv:]

        v = glu_v * _sigmoid(glu_g)
        for s in range(S):
            _to_slabs(vbuf.at[:, :, vpad:vpad + TM, :], (s,), v[s * TM:(s + 1) * TM, :])
            _strided_causal_conv(vbuf, s, dww_ref, dwb_ref, vc_buf, taps=dw_conv_w, lead=vpad - v_hist, tm=TM,
                                 rows=conv_rows)
            for c in range(d_conv // LANES):
                vbuf[s, c, 0:vpad, :] = vbuf[s, c, TM:TM + vpad, :]
        if not emit_y:
            return None

        y_rnn = _from_slabs(h_buf) * _gelu_tanh(gate)
        vc = _from_slabs(vc_buf)
        mu = jnp.mean(vc, axis=-1, keepdims=True)
        vcc = vc - mu
        ln = vcc * lax.rsqrt(jnp.mean(vcc * vcc, axis=-1, keepdims=True) + EPS) * ln_g_ref[...] + ln_b_ref[...]
        y_conv = _silu(ln)
        mix = jnp.concatenate([_rms(y_rnn, onr_ref[...]), _rms(y_conv, onc_ref[...])], axis=-1)
        return x + _bf16_dot(mix, weight("w_out")[...])

    def mixer():
        z = in_proj()
        short_conv(z)
        for j in range(n_gate_slabs):
            rglru(j, *gate_proj(j))
        return mixer_rest(z)

    n_ff = w_up_ref.shape[-1] // ff_chunk

    def mlp_chunk(hm, x2, c):
        hc = jnp.dot(hm, weight("w_up")[:, c * ff_chunk:(c + 1) * ff_chunk], preferred_element_type=jnp.float32)
        hc = jnp.square(jnp.maximum(hc, 0.0))
        return x2 + _bf16_dot(hc, weight("w_down")[c * ff_chunk:(c + 1) * ff_chunk, :])

    def mlp_begin(x1, n_chunks):
        hm = _rms(x1, norm_mlp_ref[...]).astype(jnp.bfloat16)
        x2 = x1
        for c in range(n_chunks):
            x2 = mlp_chunk(hm, x2, c)
        return hm, x2, n_chunks

    def mlp_finish(hm, x2, n_done):
        for c in range(n_done, n_ff):
            x2 = mlp_chunk(hm, x2, c)
        if final_norm:
            x2 = _rms(x2, norm_final_ref[...])
        y_ref[...] = x2.reshape(y_ref.shape)

    def store_state():
        for s in range(S):
            for c in range(d_rnn // LANES):
                conv_out_ref[s, :, c * LANES:(c + 1) * LANES] = cbuf[s, c, cpad - c_hist:cpad, :]
            for c in range(d_conv // LANES):
                dw_out_ref[s, :, c * LANES:(c + 1) * LANES] = vbuf[s, c, vpad - v_hist:vpad, :]
            h_out_ref[s] = hcar[s, 0:1, :]

    if pipelined:
        x1_buf = scratch[10]

        @pl.when(n == 0)
        def _first_step():
            start_weight_copies()
            x1_buf[...] = mixer()
            finish_weight_copies()

        @pl.when(jnp.logical_and(n > 0, live))
        def _steady_step():
            z = in_proj()
            first_gate = max(n_ff - n_gate_slabs, 0)
            hm, x2, _ = mlp_begin(x1_buf[...], first_gate)
            short_conv(z)
            for j in range(n_gate_slabs):
                gate_out = gate_proj(j)
                if first_gate + j < n_ff:
                    x2 = mlp_chunk(hm, x2, first_gate + j)
                rglru(j, *gate_out)
            mlp_finish(hm, x2, min(first_gate + n_gate_slabs, n_ff))
            x1_buf[...] = mixer_rest(z)

        @pl.when(n == n_tiles)
        def _last_step():
            mlp_finish(*mlp_begin(x1_buf[...], 0))

        pl.when(live)(store_state)
    else:
        if n_tiles == 1:
            start_weight_copies()
        else:
            @pl.when(n == 0)
            def _load_weights():
                start_weight_copies()
                finish_weight_copies()
        x1 = mixer()
        if emit_y:
            mlp_finish(*mlp_begin(x1, 0))
        finish_weight_copies()
        store_state()


def _matmul_weights(emit_y):
    return ("w_in", "w_gate") + (("w_out", "w_up", "w_down") if emit_y else ())


def _const_spec(arr):
    nd = arr.ndim
    return pl.BlockSpec(arr.shape, lambda n, _nd=nd: (0,) * _nd, pipeline_mode=pl.Buffered(1))


def _run_layer(x, conv0, h0, dw0, params, *, n_streams, tm, emit_y, final_norm):
    B, T, D = x.shape
    S, TM = n_streams, tm
    assert B % S == 0 and T % TM == 0 and TM % SUBLANES == 0
    rnn_conv_w, d_rnn = params["cw"].shape
    dw_conv_w, d_conv = params["dww"].shape
    assert d_rnn % MXU_DIM == 0 and d_conv % LANES == 0
    cpad = -(-(rnn_conv_w - 1) // SUBLANES) * SUBLANES
    vpad = -(-(dw_conv_w - 1) // SUBLANES) * SUBLANES
    conv_rows = min(64, TM)
    assert TM % conv_rows == 0 and conv_rows % (2 * SUBLANES) == 0
    d_ff = params["w_up"].shape[-1]
    ff_chunk = min(1024, d_ff)
    tps = T // TM
    n_tiles = (B // S) * tps
    pipelined = emit_y and n_tiles > 1

    def in_tile(n):
        return jnp.minimum(n, n_tiles - 1) if pipelined else n

    def out_tile(n):
        return jnp.maximum(n - 1, 0) if pipelined else n

    def state_spec(arr):
        blk = (S,) + arr.shape[1:]
        if arr.shape[0] == B:
            return pl.BlockSpec(blk, lambda n: (in_tile(n) // tps, 0, 0))
        assert arr.shape[0] == 1 and S == 1
        return pl.BlockSpec(blk, lambda n: (0, 0, 0))

    order = ["norm_mix", "w_in", "cw", "cb", "w_gate", "b_r", "b_i", "lam", "dww", "dwb", "ln_g", "ln_b",
             "onr", "onc", "w_out", "norm_mlp", "w_up", "w_down", "norm_final"]
    weights = [params[k] for k in order]
    in_specs = [pl.BlockSpec((S, TM, D), lambda n: (in_tile(n) // tps, in_tile(n) % tps, 0)),
                state_spec(conv0), state_spec(h0), state_spec(dw0)]
    in_hbm = ("w_in", "w_gate", "w_out", "w_up", "w_down")
    in_specs += [pl.BlockSpec(memory_space=pl.ANY) if k in in_hbm else _const_spec(w) for k, w in zip(order, weights)]

    f32 = jnp.float32
    out_shape = [jax.ShapeDtypeStruct((B, rnn_conv_w - 1, d_rnn), f32),
                 jax.ShapeDtypeStruct((B, 1, d_rnn), f32),
                 jax.ShapeDtypeStruct((B, dw_conv_w - 1, d_conv), f32)]
    out_specs = [pl.BlockSpec((S, rnn_conv_w - 1, d_rnn), lambda n: (in_tile(n) // tps, 0, 0)),
                 pl.BlockSpec((S, 1, d_rnn), lambda n: (in_tile(n) // tps, 0, 0)),
                 pl.BlockSpec((S, dw_conv_w - 1, d_conv), lambda n: (in_tile(n) // tps, 0, 0))]
    if emit_y:
        out_shape = [jax.ShapeDtypeStruct((B, T, D), x.dtype)] + out_shape
        out_specs = [pl.BlockSpec((S, TM, D), lambda n: (out_tile(n) // tps, out_tile(n) % tps, 0))] + out_specs

    M = S * TM
    lvl_rows = _scan_level_rows(TM)
    scratch_shapes = [
        pltpu.VMEM((S, d_rnn // LANES, cpad + TM, LANES), f32),
        pltpu.VMEM((S, d_conv // LANES, vpad + TM, LANES), f32),
        pltpu.VMEM((S, SUBLANES, d_rnn), f32),
        pltpu.VMEM((d_rnn // LANES, M, LANES), f32),
        pltpu.VMEM((d_rnn // LANES, M, LANES), f32),
        pltpu.VMEM((d_rnn // LANES, M, LANES), f32),
        pltpu.VMEM((2, S, d_rnn // LANES, max(sum(lvl_rows), SUBLANES), LANES), f32),
        pltpu.VMEM((S, d_rnn // LANES, max(sum(r + 2 * SUBLANES for r in lvl_rows), SUBLANES), LANES), f32),
        pltpu.VMEM((d_rnn // LANES, M, LANES), f32),
        pltpu.VMEM((d_conv // LANES, M, LANES), f32),
    ]
    w_names = _matmul_weights(emit_y)
    scratch_shapes += [pltpu.VMEM(params[k].shape, params[k].dtype) for k in w_names]
    scratch_shapes.append(pltpu.SemaphoreType.DMA((len(w_names),)))
    if pipelined:
        scratch_shapes.append(pltpu.VMEM((M, D), f32))
    kern = functools.partial(_layer_kernel, n_streams=S, tm=TM, n_tiles=n_tiles, tiles_per_stream=tps,
                             emit_y=emit_y, final_norm=final_norm, pipelined=pipelined,
                             rnn_conv_w=rnn_conv_w, dw_conv_w=dw_conv_w, ff_chunk=ff_chunk, conv_rows=conv_rows)
    return pl.pallas_call(
        kern,
        grid=(n_tiles + (1 if pipelined else 0),),
        in_specs=in_specs,
        out_specs=out_specs,
        out_shape=out_shape,
        scratch_shapes=scratch_shapes,
        compiler_params=pltpu.CompilerParams(
            dimension_semantics=("arbitrary",),
            vmem_limit_bytes=VMEM_LIMIT_BYTES),
        name=f"hybrid_layer_s{S}_t{TM}",
    )(x, conv0, h0, dw0, *weights)


def _block_diag_gates(w_r, w_i):
    n_heads, hd, _ = w_r.shape
    per = MXU_DIM // hd
    n_slab = n_heads // per
    on_diag = jnp.eye(per, dtype=bool)[None, :, None, :, None]

    def slabs(w):
        w = w.reshape(n_slab, per, hd, hd)
        bd = jnp.where(on_diag, w[:, :, :, None, :], 0.0)
        return bd.reshape(n_slab, MXU_DIM, MXU_DIM)

    return jnp.concatenate([slabs(w_r), slabs(w_i)], axis=-1).astype(jnp.bfloat16)


def kernel(x_prompt, x_sample, state_rglru_conv, state_rglru_h, state_dwconv, meta_tokens, norm_mix, w_in,
           rnn_conv_w, rnn_conv_b, w_gate_r, b_gate_r, w_gate_i, b_gate_i, rglru_lambda, dw_w, dw_b, ln_conv_g,
           ln_conv_b, out_norm_rnn, out_norm_conv, w_out, norm_mlp, w_up, w_down, norm_final):
    depth = norm_mix.shape[0]
    bf16 = jnp.bfloat16
    d_rnn = rnn_conv_w.shape[-1]
    d_conv = dw_w.shape[-1]
    c_hist = rnn_conv_w.shape[1] - 1
    v_hist = dw_w.shape[1] - 1

    xm = meta_tokens[None].astype(x_prompt.dtype)
    xp, xs = x_prompt, x_sample
    outs_p, outs_s = [], []
    for l in range(depth):
        last = l == depth - 1
        params = dict(
            norm_mix=norm_mix[l][None], w_in=w_in[l].astype(bf16), cw=rnn_conv_w[l], cb=rnn_conv_b[l][None],
            w_gate=_block_diag_gates(w_gate_r[l], w_gate_i[l]), b_r=b_gate_r[l][None], b_i=b_gate_i[l][None],
            lam=rglru_lambda[l][None], dww=dw_w[l], dwb=dw_b[l][None], ln_g=ln_conv_g[l][None],
            ln_b=ln_conv_b[l][None], onr=out_norm_rnn[l][None], onc=out_norm_conv[l][None],
            w_out=w_out[l].astype(bf16), norm_mlp=norm_mlp[l][None], w_up=w_up[l].astype(bf16),
            w_down=w_down[l].astype(bf16), norm_final=norm_final[None])
        f32 = jnp.float32
        zero_state = (jnp.zeros((1, c_hist, d_rnn), f32), jnp.zeros((1, 1, d_rnn), f32),
                      jnp.zeros((1, v_hist, d_conv), f32))
        res_m = _run_layer(xm, *zero_state, params, n_streams=1, tm=xm.shape[1], emit_y=not last,
                           final_norm=False)
        if not last:
            xm, res_m = res_m[0], res_m[1:]
        res_p = _run_layer(xp, *res_m, params, n_streams=1, tm=min(512, xp.shape[1]), emit_y=True,
                           final_norm=last)
        res_s = _run_layer(xs, state_rglru_conv[l], state_rglru_h[l][:, None], state_dwconv[l], params,
                           n_streams=xs.shape[0], tm=xs.shape[1], emit_y=True, final_norm=last)
        xp, xs = res_p[0], res_s[0]
        outs_p.append(res_p[1:])
        outs_s.append(res_s[1:])

    def stack(outs, i):
        return outs[0][i][None] if depth == 1 else jnp.stack([o[i] for o in outs])

    return (xp, xs,
            stack(outs_p, 0), stack(outs_p, 1)[:, :, 0], stack(outs_p, 2),
            stack(outs_s, 0), stack(outs_s, 1)[:, :, 0], stack(outs_s, 2))
```

```python
import functools
import math

import jax
import jax.numpy as jnp
from jax import lax
from jax.experimental import pallas as pl
from jax.experimental.pallas import tpu as pltpu

EPS = 1e-6
RGLRU_C = 8.0
SUBLANES = 8
LANES = 128
MXU_DIM = 256
VMEM_LIMIT_BYTES = 60000 * 1024


def _rms(x, g):
    ms = jnp.mean(x * x, axis=-1, keepdims=True)
    return x * lax.rsqrt(ms + EPS) * g


def _sigmoid(x):
    return 0.5 * jnp.tanh(0.5 * x) + 0.5


def _silu(x):
    h = 0.5 * x
    return h * (jnp.tanh(h) + 1.0)


def _gelu_tanh(x):
    c = math.sqrt(2.0 / math.pi)
    inner = x * (c + (0.044715 * c) * (x * x))
    return (0.5 * x) * (1.0 + jnp.tanh(inner))


def _bf16_dot(a, b):
    return jnp.dot(a.astype(jnp.bfloat16), b, preferred_element_type=jnp.float32)


def _to_slabs(dst, idx, x):
    for c in range(x.shape[-1] // LANES):
        dst[idx + (c,)] = x[:, c * LANES:(c + 1) * LANES]


def _from_slabs(buf):
    return jnp.concatenate([buf[c] for c in range(buf.shape[0])], axis=-1)


def _strided_causal_conv(win, s, w_ref, b_ref, out, *, taps, lead, tm, rows):
    half = rows // 2
    for c in range(win.shape[1]):
        lanes = slice(c * LANES, (c + 1) * LANES)
        for t0 in [i * rows + par for i in range(tm // rows) for par in range(2)]:
            acc = win[s, c, pl.ds(lead + t0, half, stride=2), :] * w_ref[0:1, lanes]
            for k in range(1, taps):
                acc = acc + win[s, c, pl.ds(lead + t0 + k, half, stride=2), :] * w_ref[k:k + 1, lanes]
            out[c, pl.ds(s * tm + t0, half, stride=2), :] = acc + b_ref[:, lanes]


def _scan_radices(n_rows):
    n_groups, radices = n_rows // SUBLANES, []
    while n_groups > 1:
        assert n_groups % 2 == 0
        r = 4 if n_groups % 4 == 0 else 2
        radices.append(r)
        n_groups //= r
    return radices


def _scan_level_rows(n_rows):
    rows, out = n_rows, []
    for r in _scan_radices(n_rows)[:-1]:
        rows //= r
        out.append(rows)
    return out


def _linear_scan_slab(a_ref, b_ref, h_ref, ab_lvl, hin_lvl, base, n_rows, carry):
    radices = _scan_radices(n_rows)
    lvl_rows = _scan_level_rows(n_rows)
    ab_off = [sum(lvl_rows[:i]) for i in range(len(lvl_rows))]
    hin_off = [sum(r + 2 * SUBLANES for r in lvl_rows[:i]) for i in range(len(lvl_rows))]

    partial = []
    rows = n_rows
    for lvl, r in enumerate(radices):
        g = rows // r
        src, off = ((a_ref, b_ref), base) if lvl == 0 else ((ab_lvl.at[0], ab_lvl.at[1]), ab_off[lvl - 1])
        ld = lambda kind, k: src[kind][pl.ds(off + k, g, stride=r), :]
        pa, pb = ld(0, 0), ld(1, 0)
        maps = [(pa, pb)]
        for k in range(1, r):
            ak, bk = ld(0, k), ld(1, k)
            pa, pb = ak * pa, ak * pb + bk
            maps.append((pa, pb))
        partial.append(maps)
        rows = g
        if lvl + 1 < len(radices):
            ab_lvl[0, ab_off[lvl]:ab_off[lvl] + g, :] = pa
            ab_lvl[1, ab_off[lvl]:ab_off[lvl] + g, :] = pb
    if not radices:
        pa, pb = a_ref[base:base + SUBLANES, :], b_ref[base:base + SUBLANES, :]

    row = lax.broadcasted_iota(jnp.int32, (SUBLANES, LANES), 0)
    for sh in (1, 2, 4):
        keep = row >= sh
        pa_s = jnp.where(keep, pltpu.roll(pa, sh, 0), 1.0)
        pb_s = jnp.where(keep, pltpu.roll(pb, sh, 0), 0.0)
        pa, pb = pa * pa_s, pa * pb_s + pb
    h_top = pa * carry + pb
    new_carry = jnp.broadcast_to(h_top[SUBLANES - 1:SUBLANES, :], (SUBLANES, LANES))
    if not radices:
        h_ref[base:base + SUBLANES, :] = h_top
        return new_carry
    h_prev = jnp.where(row == 0, carry, pltpu.roll(h_top, 1, 0))

    for lvl in reversed(range(len(radices))):
        r = radices[lvl]
        g = partial[lvl][0][0].shape[0]
        if lvl < len(radices) - 1:
            o = hin_off[lvl] + SUBLANES
            h_prev = hin_lvl[o:o + g, :]
        for k, (pa, pb) in enumerate(partial[lvl]):
            hk = pa * h_prev + pb
            if lvl == 0:
                h_ref[pl.ds(base + k, g, stride=r), :] = hk
            else:
                o = hin_off[lvl - 1] + SUBLANES
                hin_lvl[pl.ds(o + 1 + k, g, stride=r), :] = hk
        if lvl > 0:
            o = hin_off[lvl - 1] + SUBLANES
            hin_lvl[o:o + 1, :] = carry[0:1, :]
    return new_carry


def _layer_kernel(x_ref, conv0_ref, h0_ref, dw0_ref,
                  norm_mix_ref, w_in_ref, cw_ref, cb_ref, w_gate_ref, b_r_ref, b_i_ref, lam_ref,
                  dww_ref, dwb_ref, ln_g_ref, ln_b_ref, onr_ref, onc_ref,
                  w_out_ref, norm_mlp_ref, w_up_ref, w_down_ref, norm_final_ref,
                  *rest, n_streams, tm, n_tiles, tiles_per_stream, emit_y, final_norm, pipelined,
                  rnn_conv_w, dw_conv_w, ff_chunk, conv_rows, zero_state):
    if emit_y:
        y_ref, conv_out_ref, h_out_ref, dw_out_ref = rest[:4]
        scratch = rest[4:]
    else:
        conv_out_ref, h_out_ref, dw_out_ref = rest[:3]
        scratch = rest[3:]
    cbuf, vbuf, hcar, a_buf, b_buf, h_buf, ab_lvl, hin_lvl, xc_buf, vc_buf = scratch[:10]

    w_hbm = dict(w_in=w_in_ref, w_gate=w_gate_ref, w_out=w_out_ref, w_up=w_up_ref, w_down=w_down_ref)
    w_names = _matmul_weights(emit_y)
    w_vmem = dict(zip(w_names, scratch[10:10 + len(w_names)]))
    w_sem = scratch[10 + len(w_names)]
    scratch = scratch[:10] + scratch[11 + len(w_names):]
    in_flight = set()

    def weight_copy(name):
        return pltpu.make_async_copy(w_hbm[name], w_vmem[name], w_sem.at[w_names.index(name)])

    def start_weight_copies():
        for name in w_names:
            weight_copy(name).start()
            in_flight.add(name)

    def finish_weight_copies():
        for name in w_names:
            weight(name)

    def weight(name):
        if name in in_flight:
            weight_copy(name).wait()
            in_flight.discard(name)
        return w_vmem[name]

    S, TM = n_streams, tm
    M = S * TM
    d_rnn = cw_ref.shape[-1]
    d_conv = dww_ref.shape[-1]
    cpad = cbuf.shape[2] - TM
    vpad = vbuf.shape[2] - TM
    c_hist = rnn_conv_w - 1
    v_hist = dw_conv_w - 1
    n = pl.program_id(0)
    tile = jnp.minimum(n, n_tiles - 1)
    live = n < n_tiles

    @pl.when(jnp.logical_and(lax.rem(tile, tiles_per_stream) == 0, live))
    def _load_state():
        if zero_state:
            cbuf[:, :, 0:cpad, :] = jnp.zeros((S, d_rnn // LANES, cpad, LANES), jnp.float32)
            vbuf[:, :, 0:vpad, :] = jnp.zeros((S, d_conv // LANES, vpad, LANES), jnp.float32)
            hcar[...] = jnp.zeros(hcar.shape, jnp.float32)
            return
        cbuf[:, :, 0:cpad - c_hist, :] = jnp.zeros((S, d_rnn // LANES, cpad - c_hist, LANES), jnp.float32)
        vbuf[:, :, 0:vpad - v_hist, :] = jnp.zeros((S, d_conv // LANES, vpad - v_hist, LANES), jnp.float32)
        for c in range(d_rnn // LANES):
            cbuf[:, c, cpad - c_hist:cpad, :] = conv0_ref[:, :, c * LANES:(c + 1) * LANES]
        for c in range(d_conv // LANES):
            vbuf[:, c, vpad - v_hist:vpad, :] = dw0_ref[:, :, c * LANES:(c + 1) * LANES]
        hcar[...] = jnp.broadcast_to(h0_ref[...], hcar.shape)

    def in_proj():
        x = x_ref[...].reshape(M, x_ref.shape[-1])
        hn = _rms(x, norm_mix_ref[...])
        return _bf16_dot(hn, weight("w_in")[...])

    n_gate_slabs = d_rnn // MXU_DIM
    lanes_per_gate_slab = MXU_DIM // LANES

    def short_conv(z):
        xr = z[:, 0:d_rnn]
        for s in range(S):
            _to_slabs(cbuf.at[:, :, cpad:cpad + TM, :], (s,), xr[s * TM:(s + 1) * TM, :])
            _strided_causal_conv(cbuf, s, cw_ref, cb_ref, xc_buf, taps=rnn_conv_w, lead=cpad - c_hist, tm=TM,
                                 rows=conv_rows)
            for c in range(d_rnn // LANES):
                cbuf[s, c, 0:cpad, :] = cbuf[s, c, TM:TM + cpad, :]

    def gate_proj(j):
        slabs = range(j * lanes_per_gate_slab, (j + 1) * lanes_per_gate_slab)
        xc = jnp.concatenate([xc_buf[c] for c in slabs], axis=-1)
        return xc, _bf16_dot(xc, weight("w_gate")[j])

    def rglru(j, xc, ri):
        lanes = slice(j * MXU_DIM, (j + 1) * MXU_DIM)
        r = _sigmoid(ri[:, :MXU_DIM] + b_r_ref[:, lanes])
        ig = _sigmoid(ri[:, MXU_DIM:] + b_i_ref[:, lanes])
        nl = -lam_ref[:, lanes]
        softplus_nl = jnp.maximum(nl, 0.0) + jnp.log1p(jnp.exp(-jnp.abs(nl)))
        a = jnp.exp((-RGLRU_C * r) * softplus_nl)
        b = jnp.sqrt(jnp.maximum(1.0 - a * a, 0.0)) * ig * xc
        for k in range(lanes_per_gate_slab):
            c = j * lanes_per_gate_slab + k
            a_buf[c] = a[:, k * LANES:(k + 1) * LANES]
            b_buf[c] = b[:, k * LANES:(k + 1) * LANES]
            for s in range(S):
                hcar[s, :, c * LANES:(c + 1) * LANES] = _linear_scan_slab(
                    a_buf.at[c], b_buf.at[c], h_buf.at[c], ab_lvl.at[:, s, c], hin_lvl.at[s, c], s * TM, TM,
                    hcar[s, :, c * LANES:(c + 1) * LANES])

    def mixer_rest(z):
        x = x_ref[...].reshape(M, x_ref.shape[-1])
        gate = z[:, d_rnn:2 * d_rnn]
        glu_v = z[:, 2 * d_rnn:2 * d_rnn + d_conv]
        glu_g = z[:, 2 * d_rnn + d_conv:]

        v = glu_v * _sigmoid(glu_g)
        for s in range(S):
            _to_slabs(vbuf.at[:, :, vpad:vpad + TM, :], (s,), v[s * TM:(s + 1) * TM, :])
            _strided_causal_conv(vbuf, s, dww_ref, dwb_ref, vc_buf, taps=dw_conv_w, lead=vpad - v_hist, tm=TM,
                                 rows=conv_rows)
            for c in range(d_conv // LANES):
                vbuf[s, c, 0:vpad, :] = vbuf[s, c, TM:TM + vpad, :]
        if not emit_y:
            return None

        y_rnn = _from_slabs(h_buf) * _gelu_tanh(gate)
        vc = _from_slabs(vc_buf)
        mu = jnp.mean(vc, axis=-1, keepdims=True)
        vcc = vc - mu
        ln = vcc * lax.rsqrt(jnp.mean(vcc * vcc, axis=-1, keepdims=True) + EPS) * ln_g_ref[...] + ln_b_ref[...]
        y_conv = _silu(ln)
        mix = jnp.concatenate([_rms(y_rnn, onr_ref[...]), _rms(y_conv, onc_ref[...])], axis=-1)
        return x + _bf16_dot(mix, weight("w_out")[...])

    def mixer():
        z = in_proj()
        short_conv(z)
        for j in range(n_gate_slabs):
            rglru(j, *gate_proj(j))
        return mixer_rest(z)

    n_ff = w_up_ref.shape[-1] // ff_chunk

    def mlp_chunk(hm, x2, c):
        hc = jnp.dot(hm, weight("w_up")[:, c * ff_chunk:(c + 1) * ff_chunk], preferred_element_type=jnp.float32)
        hc = jnp.square(jnp.maximum(hc, 0.0))
        return x2 + _bf16_dot(hc, weight("w_down")[c * ff_chunk:(c + 1) * ff_chunk, :])

    def mlp_begin(x1, n_chunks):
        hm = _rms(x1, norm_mlp_ref[...]).astype(jnp.bfloat16)
        x2 = x1
        for c in range(n_chunks):
            x2 = mlp_chunk(hm, x2, c)
        return hm, x2, n_chunks

    def mlp_finish(hm, x2, n_done):
        for c in range(n_done, n_ff):
            x2 = mlp_chunk(hm, x2, c)
        if final_norm:
            x2 = _rms(x2, norm_final_ref[...])
        y_ref[...] = x2.reshape(y_ref.shape)

    def store_state():
        for s in range(S):
            for c in range(d_rnn // LANES):
                conv_out_ref[s, :, c * LANES:(c + 1) * LANES] = cbuf[s, c, cpad - c_hist:cpad, :]
            for c in range(d_conv // LANES):
                dw_out_ref[s, :, c * LANES:(c + 1) * LANES] = vbuf[s, c, vpad - v_hist:vpad, :]
            h_out_ref[s] = hcar[s, 0:1, :]

    if pipelined:
        x1_buf = scratch[10]

        @pl.when(n == 0)
        def _first_step():
            start_weight_copies()
            x1_buf[...] = mixer()
            finish_weight_copies()

        @pl.when(jnp.logical_and(n > 0, live))
        def _steady_step():
            z = in_proj()
            first_gate = max(n_ff - n_gate_slabs, 0)
            hm, x2, _ = mlp_begin(x1_buf[...], first_gate)
            short_conv(z)
            for j in range(n_gate_slabs):
                gate_out = gate_proj(j)
                if first_gate + j < n_ff:
                    x2 = mlp_chunk(hm, x2, first_gate + j)
                rglru(j, *gate_out)
            mlp_finish(hm, x2, min(first_gate + n_gate_slabs, n_ff))
            x1_buf[...] = mixer_rest(z)

        @pl.when(n == n_tiles)
        def _last_step():
            mlp_finish(*mlp_begin(x1_buf[...], 0))

        pl.when(live)(store_state)
    else:
        if n_tiles == 1:
            start_weight_copies()
        else:
            @pl.when(n == 0)
            def _load_weights():
                start_weight_copies()
                finish_weight_copies()
        x1 = mixer()
        if emit_y:
            mlp_finish(*mlp_begin(x1, 0))
        finish_weight_copies()
        store_state()


def _matmul_weights(emit_y):
    return ("w_in", "w_gate") + (("w_out", "w_up", "w_down") if emit_y else ())


def _const_spec(arr):
    nd = arr.ndim
    return pl.BlockSpec(arr.shape, lambda n, _nd=nd: (0,) * _nd, pipeline_mode=pl.Buffered(1))


def _run_layer(x, conv0, h0, dw0, params, *, n_streams, tm, emit_y, final_norm):
    B, T, D = x.shape
    S, TM = n_streams, tm
    assert B % S == 0 and T % TM == 0 and TM % SUBLANES == 0
    rnn_conv_w, d_rnn = params["cw"].shape
    dw_conv_w, d_conv = params["dww"].shape
    assert d_rnn % MXU_DIM == 0 and d_conv % LANES == 0
    cpad = -(-(rnn_conv_w - 1) // SUBLANES) * SUBLANES
    vpad = -(-(dw_conv_w - 1) // SUBLANES) * SUBLANES
    conv_rows = min(64, TM)
    assert TM % conv_rows == 0 and conv_rows % (2 * SUBLANES) == 0
    d_ff = params["w_up"].shape[-1]
    ff_chunk = min(1024, d_ff)
    tps = T // TM
    n_tiles = (B // S) * tps
    pipelined = emit_y and n_tiles > 1

    def in_tile(n):
        return jnp.minimum(n, n_tiles - 1) if pipelined else n

    def out_tile(n):
        return jnp.maximum(n - 1, 0) if pipelined else n

    zero_state = conv0 is None
    if zero_state:
        assert h0 is None and dw0 is None
        conv0 = h0 = dw0 = params["cb"]

    def state_spec(arr):
        if zero_state:
            return pl.BlockSpec(memory_space=pl.ANY)
        blk = (S,) + arr.shape[1:]
        if arr.shape[0] == B:
            return pl.BlockSpec(blk, lambda n: (in_tile(n) // tps, 0, 0))
        assert arr.shape[0] == 1 and S == 1
        return pl.BlockSpec(blk, lambda n: (0, 0, 0))

    order = ["norm_mix", "w_in", "cw", "cb", "w_gate", "b_r", "b_i", "lam", "dww", "dwb", "ln_g", "ln_b",
             "onr", "onc", "w_out", "norm_mlp", "w_up", "w_down", "norm_final"]
    weights = [params[k] for k in order]
    in_specs = [pl.BlockSpec((S, TM, D), lambda n: (in_tile(n) // tps, in_tile(n) % tps, 0)),
                state_spec(conv0), state_spec(h0), state_spec(dw0)]
    in_hbm = ("w_in", "w_gate", "w_out", "w_up", "w_down")
    in_specs += [pl.BlockSpec(memory_space=pl.ANY) if k in in_hbm else _const_spec(w) for k, w in zip(order, weights)]

    f32 = jnp.float32
    out_shape = [jax.ShapeDtypeStruct((B, rnn_conv_w - 1, d_rnn), f32),
                 jax.ShapeDtypeStruct((B, 1, d_rnn), f32),
                 jax.ShapeDtypeStruct((B, dw_conv_w - 1, d_conv), f32)]
    out_specs = [pl.BlockSpec((S, rnn_conv_w - 1, d_rnn), lambda n: (in_tile(n) // tps, 0, 0)),
                 pl.BlockSpec((S, 1, d_rnn), lambda n: (in_tile(n) // tps, 0, 0)),
                 pl.BlockSpec((S, dw_conv_w - 1, d_conv), lambda n: (in_tile(n) // tps, 0, 0))]
    if emit_y:
        out_shape = [jax.ShapeDtypeStruct((B, T, D), x.dtype)] + out_shape
        out_specs = [pl.BlockSpec((S, TM, D), lambda n: (out_tile(n) // tps, out_tile(n) % tps, 0))] + out_specs

    M = S * TM
    lvl_rows = _scan_level_rows(TM)
    scratch_shapes = [
        pltpu.VMEM((S, d_rnn // LANES, cpad + TM, LANES), f32),
        pltpu.VMEM((S, d_conv // LANES, vpad + TM, LANES), f32),
        pltpu.VMEM((S, SUBLANES, d_rnn), f32),
        pltpu.VMEM((d_rnn // LANES, M, LANES), f32),
        pltpu.VMEM((d_rnn // LANES, M, LANES), f32),
        pltpu.VMEM((d_rnn // LANES, M, LANES), f32),
        pltpu.VMEM((2, S, d_rnn // LANES, max(sum(lvl_rows), SUBLANES), LANES), f32),
        pltpu.VMEM((S, d_rnn // LANES, max(sum(r + 2 * SUBLANES for r in lvl_rows), SUBLANES), LANES), f32),
        pltpu.VMEM((d_rnn // LANES, M, LANES), f32),
        pltpu.VMEM((d_conv // LANES, M, LANES), f32),
    ]
    w_names = _matmul_weights(emit_y)
    scratch_shapes += [pltpu.VMEM(params[k].shape, params[k].dtype) for k in w_names]
    scratch_shapes.append(pltpu.SemaphoreType.DMA((len(w_names),)))
    if pipelined:
        scratch_shapes.append(pltpu.VMEM((M, D), f32))
    kern = functools.partial(_layer_kernel, n_streams=S, tm=TM, n_tiles=n_tiles, tiles_per_stream=tps,
                             emit_y=emit_y, final_norm=final_norm, pipelined=pipelined,
                             rnn_conv_w=rnn_conv_w, dw_conv_w=dw_conv_w, ff_chunk=ff_chunk, conv_rows=conv_rows,
                             zero_state=zero_state)
    return pl.pallas_call(
        kern,
        grid=(n_tiles + (1 if pipelined else 0),),
        in_specs=in_specs,
        out_specs=out_specs,
        out_shape=out_shape,
        scratch_shapes=scratch_shapes,
        compiler_params=pltpu.CompilerParams(
            dimension_semantics=("arbitrary",),
            vmem_limit_bytes=VMEM_LIMIT_BYTES),
        name=f"hybrid_layer_s{S}_t{TM}",
    )(x, conv0, h0, dw0, *weights)


def _block_diag_gates(w_r, w_i):
    n_heads, hd, _ = w_r.shape
    per = MXU_DIM // hd
    n_slab = n_heads // per
    on_diag = jnp.eye(per, dtype=bool)[None, :, None, :, None]

    def slabs(w):
        w = w.reshape(n_slab, per, hd, hd)
        bd = jnp.where(on_diag, w[:, :, :, None, :], 0.0)
        return bd.reshape(n_slab, MXU_DIM, MXU_DIM)

    return jnp.concatenate([slabs(w_r), slabs(w_i)], axis=-1).astype(jnp.bfloat16)


def kernel(x_prompt, x_sample, state_rglru_conv, state_rglru_h, state_dwconv, meta_tokens, norm_mix, w_in,
           rnn_conv_w, rnn_conv_b, w_gate_r, b_gate_r, w_gate_i, b_gate_i, rglru_lambda, dw_w, dw_b, ln_conv_g,
           ln_conv_b, out_norm_rnn, out_norm_conv, w_out, norm_mlp, w_up, w_down, norm_final):
    depth = norm_mix.shape[0]
    bf16 = jnp.bfloat16

    xm = meta_tokens[None].astype(x_prompt.dtype)
    xp, xs = x_prompt, x_sample
    outs_p, outs_s = [], []
    for l in range(depth):
        last = l == depth - 1
        params = dict(
            norm_mix=norm_mix[l][None], w_in=w_in[l].astype(bf16), cw=rnn_conv_w[l], cb=rnn_conv_b[l][None],
            w_gate=_block_diag_gates(w_gate_r[l], w_gate_i[l]), b_r=b_gate_r[l][None], b_i=b_gate_i[l][None],
            lam=rglru_lambda[l][None], dww=dw_w[l], dwb=dw_b[l][None], ln_g=ln_conv_g[l][None],
            ln_b=ln_conv_b[l][None], onr=out_norm_rnn[l][None], onc=out_norm_conv[l][None],
            w_out=w_out[l].astype(bf16), norm_mlp=norm_mlp[l][None], w_up=w_up[l].astype(bf16),
            w_down=w_down[l].astype(bf16), norm_final=norm_final[None])
        res_m = _run_layer(xm, None, None, None, params, n_streams=1, tm=xm.shape[1], emit_y=not last,
                           final_norm=False)
        if not last:
            xm, res_m = res_m[0], res_m[1:]
        res_p = _run_layer(xp, *res_m, params, n_streams=1, tm=min(512, xp.shape[1]), emit_y=True,
                           final_norm=last)
        res_s = _run_layer(xs, state_rglru_conv[l], state_rglru_h[l][:, None], state_dwconv[l], params,
                           n_streams=xs.shape[0], tm=xs.shape[1], emit_y=True, final_norm=last)
        xp, xs = res_p[0], res_s[0]
        outs_p.append(res_p[1:])
        outs_s.append(res_s[1:])

    def stack(outs, i):
        return outs[0][i][None] if depth == 1 else jnp.stack([o[i] for o in outs])

    return (xp, xs,
            stack(outs_p, 0), stack(outs_p, 1)[:, :, 0], stack(outs_p, 2),
            stack(outs_s, 0), stack(outs_s, 1)[:, :, 0], stack(outs_s, 2))
```

```python
import functools
import math

import jax
import jax.numpy as jnp
from jax import lax
from jax.experimental import pallas as pl
from jax.experimental.pallas import tpu as pltpu

EPS = 1e-6
RGLRU_C = 8.0
SUBLANES = 8
LANES = 128
MXU_DIM = 256
VMEM_LIMIT_BYTES = 60000 * 1024


def _rms(x, g):
    ms = jnp.mean(x * x, axis=-1, keepdims=True)
    return x * lax.rsqrt(ms + EPS) * g


def _sigmoid(x):
    return 0.5 * jnp.tanh(0.5 * x) + 0.5


def _silu(x):
    h = 0.5 * x
    return h * (jnp.tanh(h) + 1.0)


def _gelu_tanh(x):
    c = math.sqrt(2.0 / math.pi)
    inner = x * (c + (0.044715 * c) * (x * x))
    return (0.5 * x) * (1.0 + jnp.tanh(inner))


def _bf16_dot(a, b):
    return jnp.dot(a.astype(jnp.bfloat16), b, preferred_element_type=jnp.float32)


def _to_slabs(dst, idx, x):
    for c in range(x.shape[-1] // LANES):
        dst[idx + (c,)] = x[:, c * LANES:(c + 1) * LANES]


def _from_slabs(buf):
    return jnp.concatenate([buf[c] for c in range(buf.shape[0])], axis=-1)


def _strided_causal_conv(win, s, w_ref, b_ref, out, *, taps, lead, tm, rows):
    half = rows // 2
    for c in range(win.shape[1]):
        lanes = slice(c * LANES, (c + 1) * LANES)
        for t0 in [i * rows + par for i in range(tm // rows) for par in range(2)]:
            acc = win[s, c, pl.ds(lead + t0, half, stride=2), :] * w_ref[0:1, lanes]
            for k in range(1, taps):
                acc = acc + win[s, c, pl.ds(lead + t0 + k, half, stride=2), :] * w_ref[k:k + 1, lanes]
            out[c, pl.ds(s * tm + t0, half, stride=2), :] = acc + b_ref[:, lanes]


def _scan_radices(n_rows):
    n_groups, radices = n_rows // SUBLANES, []
    while n_groups > 1:
        assert n_groups % 2 == 0
        r = 4 if n_groups % 4 == 0 else 2
        radices.append(r)
        n_groups //= r
    return radices


def _scan_level_rows(n_rows):
    rows, out = n_rows, []
    for r in _scan_radices(n_rows)[:-1]:
        rows //= r
        out.append(rows)
    return out


def _linear_scan_slab(a_ref, b_ref, h_ref, ab_lvl, hin_lvl, base, n_rows, carry):
    radices = _scan_radices(n_rows)
    lvl_rows = _scan_level_rows(n_rows)
    ab_off = [sum(lvl_rows[:i]) for i in range(len(lvl_rows))]
    hin_off = [sum(r + 2 * SUBLANES for r in lvl_rows[:i]) for i in range(len(lvl_rows))]

    partial = []
    rows = n_rows
    for lvl, r in enumerate(radices):
        g = rows // r
        src, off = ((a_ref, b_ref), base) if lvl == 0 else ((ab_lvl.at[0], ab_lvl.at[1]), ab_off[lvl - 1])
        ld = lambda kind, k: src[kind][pl.ds(off + k, g, stride=r), :]
        pa, pb = ld(0, 0), ld(1, 0)
        maps = [(pa, pb)]
        for k in range(1, r):
            ak, bk = ld(0, k), ld(1, k)
            pa, pb = ak * pa, ak * pb + bk
            maps.append((pa, pb))
        partial.append(maps)
        rows = g
        if lvl + 1 < len(radices):
            ab_lvl[0, ab_off[lvl]:ab_off[lvl] + g, :] = pa
            ab_lvl[1, ab_off[lvl]:ab_off[lvl] + g, :] = pb
    if not radices:
        pa, pb = a_ref[base:base + SUBLANES, :], b_ref[base:base + SUBLANES, :]

    row = lax.broadcasted_iota(jnp.int32, (SUBLANES, LANES), 0)
    for sh in (1, 2, 4):
        keep = row >= sh
        pa_s = jnp.where(keep, pltpu.roll(pa, sh, 0), 1.0)
        pb_s = jnp.where(keep, pltpu.roll(pb, sh, 0), 0.0)
        pa, pb = pa * pa_s, pa * pb_s + pb
    h_top = pa * carry + pb
    new_carry = jnp.broadcast_to(h_top[SUBLANES - 1:SUBLANES, :], (SUBLANES, LANES))
    if not radices:
        h_ref[base:base + SUBLANES, :] = h_top
        return new_carry
    h_prev = jnp.where(row == 0, carry, pltpu.roll(h_top, 1, 0))

    for lvl in reversed(range(len(radices))):
        r = radices[lvl]
        g = partial[lvl][0][0].shape[0]
        if lvl < len(radices) - 1:
            o = hin_off[lvl] + SUBLANES
            h_prev = hin_lvl[o:o + g, :]
        for k, (pa, pb) in enumerate(partial[lvl]):
            hk = pa * h_prev + pb
            if lvl == 0:
                h_ref[pl.ds(base + k, g, stride=r), :] = hk
            else:
                o = hin_off[lvl - 1] + SUBLANES
                hin_lvl[pl.ds(o + 1 + k, g, stride=r), :] = hk
        if lvl > 0:
            o = hin_off[lvl - 1] + SUBLANES
            hin_lvl[o:o + 1, :] = carry[0:1, :]
    return new_carry


def _layer_kernel(x_ref, conv0_ref, h0_ref, dw0_ref,
                  norm_mix_ref, w_in_ref, cw_ref, cb_ref, w_gate_ref, b_r_ref, b_i_ref, lam_ref,
                  dww_ref, dwb_ref, ln_g_ref, ln_b_ref, onr_ref, onc_ref,
                  w_out_ref, norm_mlp_ref, w_up_ref, w_down_ref, norm_final_ref,
                  *rest, n_streams, tm, n_tiles, tiles_per_stream, emit_y, final_norm, pipelined,
                  rnn_conv_w, dw_conv_w, ff_chunk, conv_rows, zero_state):
    if emit_y:
        y_ref, conv_out_ref, h_out_ref, dw_out_ref = rest[:4]
        scratch = rest[4:]
    else:
        conv_out_ref, h_out_ref, dw_out_ref = rest[:3]
        scratch = rest[3:]
    cbuf, vbuf, hcar, a_buf, b_buf, h_buf, ab_lvl, hin_lvl, xc_buf, vc_buf = scratch[:10]

    w_hbm = dict(w_in=w_in_ref, w_gate=w_gate_ref, w_out=w_out_ref, w_up=w_up_ref, w_down=w_down_ref)
    w_names = _matmul_weights(emit_y)
    w_vmem = dict(zip(w_names, scratch[10:10 + len(w_names)]))
    w_sem = scratch[10 + len(w_names)]
    scratch = scratch[:10] + scratch[11 + len(w_names):]
    in_flight = set()

    def weight_copy(name):
        return pltpu.make_async_copy(w_hbm[name], w_vmem[name], w_sem.at[w_names.index(name)])

    def start_weight_copies():
        for name in w_names:
            weight_copy(name).start()
            in_flight.add(name)

    def finish_weight_copies():
        for name in w_names:
            weight(name)

    def weight(name):
        if name in in_flight:
            weight_copy(name).wait()
            in_flight.discard(name)
        return w_vmem[name]

    S, TM = n_streams, tm
    M = S * TM
    d_rnn = cw_ref.shape[-1]
    d_conv = dww_ref.shape[-1]
    cpad = cbuf.shape[2] - TM
    vpad = vbuf.shape[2] - TM
    c_hist = rnn_conv_w - 1
    v_hist = dw_conv_w - 1
    n = pl.program_id(0)
    tile = jnp.minimum(n, n_tiles - 1)
    live = n < n_tiles

    @pl.when(jnp.logical_and(lax.rem(tile, tiles_per_stream) == 0, live))
    def _load_state():
        if zero_state:
            cbuf[:, :, 0:cpad, :] = jnp.zeros((S, d_rnn // LANES, cpad, LANES), jnp.float32)
            vbuf[:, :, 0:vpad, :] = jnp.zeros((S, d_conv // LANES, vpad, LANES), jnp.float32)
            hcar[...] = jnp.zeros(hcar.shape, jnp.float32)
            return
        cbuf[:, :, 0:cpad - c_hist, :] = jnp.zeros((S, d_rnn // LANES, cpad - c_hist, LANES), jnp.float32)
        vbuf[:, :, 0:vpad - v_hist, :] = jnp.zeros((S, d_conv // LANES, vpad - v_hist, LANES), jnp.float32)
        for c in range(d_rnn // LANES):
            cbuf[:, c, cpad - c_hist:cpad, :] = conv0_ref[:, :, c * LANES:(c + 1) * LANES]
        for c in range(d_conv // LANES):
            vbuf[:, c, vpad - v_hist:vpad, :] = dw0_ref[:, :, c * LANES:(c + 1) * LANES]
        hcar[...] = jnp.broadcast_to(h0_ref[...], hcar.shape)

    def in_proj():
        x = x_ref[...].reshape(M, x_ref.shape[-1])
        hn = _rms(x, norm_mix_ref[...])
        return _bf16_dot(hn, weight("w_in")[...])

    n_gate_slabs = d_rnn // MXU_DIM
    lanes_per_gate_slab = MXU_DIM // LANES

    def short_conv(z):
        xr = z[:, 0:d_rnn]
        for s in range(S):
            _to_slabs(cbuf.at[:, :, cpad:cpad + TM, :], (s,), xr[s * TM:(s + 1) * TM, :])
            _strided_causal_conv(cbuf, s, cw_ref, cb_ref, xc_buf, taps=rnn_conv_w, lead=cpad - c_hist, tm=TM,
                                 rows=conv_rows)
            for c in range(d_rnn // LANES):
                cbuf[s, c, 0:cpad, :] = cbuf[s, c, TM:TM + cpad, :]

    def gate_proj(j):
        slabs = range(j * lanes_per_gate_slab, (j + 1) * lanes_per_gate_slab)
        xc = jnp.concatenate([xc_buf[c] for c in slabs], axis=-1)
        return xc, _bf16_dot(xc, weight("w_gate")[j])

    def rglru(j, xc, ri):
        lanes = slice(j * MXU_DIM, (j + 1) * MXU_DIM)
        r = _sigmoid(ri[:, :MXU_DIM] + b_r_ref[:, lanes])
        ig = _sigmoid(ri[:, MXU_DIM:] + b_i_ref[:, lanes])
        nl = -lam_ref[:, lanes]
        softplus_nl = jnp.maximum(nl, 0.0) + jnp.log1p(jnp.exp(-jnp.abs(nl)))
        a = jnp.exp((-RGLRU_C * r) * softplus_nl)
        b = jnp.sqrt(jnp.maximum(1.0 - a * a, 0.0)) * ig * xc
        for k in range(lanes_per_gate_slab):
            c = j * lanes_per_gate_slab + k
            a_buf[c] = a[:, k * LANES:(k + 1) * LANES]
            b_buf[c] = b[:, k * LANES:(k + 1) * LANES]
            for s in range(S):
                hcar[s, :, c * LANES:(c + 1) * LANES] = _linear_scan_slab(
                    a_buf.at[c], b_buf.at[c], h_buf.at[c], ab_lvl.at[:, s, c], hin_lvl.at[s, c], s * TM, TM,
                    hcar[s, :, c * LANES:(c + 1) * LANES])

    def mixer_rest(z):
        x = x_ref[...].reshape(M, x_ref.shape[-1])
        gate = z[:, d_rnn:2 * d_rnn]
        glu_v = z[:, 2 * d_rnn:2 * d_rnn + d_conv]
        glu_g = z[:, 2 * d_rnn + d_conv:]

        v = glu_v * _sigmoid(glu_g)
        for s in range(S):
            _to_slabs(vbuf.at[:, :, vpad:vpad + TM, :], (s,), v[s * TM:(s + 1) * TM, :])
            _strided_causal_conv(vbuf, s, dww_ref, dwb_ref, vc_buf, taps=dw_conv_w, lead=vpad - v_hist, tm=TM,
                                 rows=conv_rows)
            for c in range(d_conv // LANES):
                vbuf[s, c, 0:vpad, :] = vbuf[s, c, TM:TM + vpad, :]
        if not emit_y:
            return None

        y_rnn = _from_slabs(h_buf) * _gelu_tanh(gate)
        vc = _from_slabs(vc_buf)
        mu = jnp.mean(vc, axis=-1, keepdims=True)
        vcc = vc - mu
        ln = vcc * lax.rsqrt(jnp.mean(vcc * vcc, axis=-1, keepdims=True) + EPS) * ln_g_ref[...] + ln_b_ref[...]
        y_conv = _silu(ln)
        mix = jnp.concatenate([_rms(y_rnn, onr_ref[...]), _rms(y_conv, onc_ref[...])], axis=-1)
        return x + _bf16_dot(mix, weight("w_out")[...])

    def mixer():
        z = in_proj()
        short_conv(z)
        for j in range(n_gate_slabs):
            rglru(j, *gate_proj(j))
        return mixer_rest(z)

    n_ff = w_up_ref.shape[-1] // ff_chunk

    def mlp_chunk(hm, x2, c):
        hc = jnp.dot(hm, weight("w_up")[:, c * ff_chunk:(c + 1) * ff_chunk], preferred_element_type=jnp.float32)
        hc = jnp.square(jnp.maximum(hc, 0.0))
        return x2 + _bf16_dot(hc, weight("w_down")[c * ff_chunk:(c + 1) * ff_chunk, :])

    def mlp_begin(x1, n_chunks):
        hm = _rms(x1, norm_mlp_ref[...]).astype(jnp.bfloat16)
        x2 = x1
        for c in range(n_chunks):
            x2 = mlp_chunk(hm, x2, c)
        return hm, x2, n_chunks

    def mlp_finish(hm, x2, n_done):
        for c in range(n_done, n_ff):
            x2 = mlp_chunk(hm, x2, c)
        if final_norm:
            x2 = _rms(x2, norm_final_ref[...])
        y_ref[...] = x2.reshape(y_ref.shape)

    def store_state():
        for s in range(S):
            for c in range(d_rnn // LANES):
                conv_out_ref[s, :, c * LANES:(c + 1) * LANES] = cbuf[s, c, cpad - c_hist:cpad, :]
            for c in range(d_conv // LANES):
                dw_out_ref[s, :, c * LANES:(c + 1) * LANES] = vbuf[s, c, vpad - v_hist:vpad, :]
            h_out_ref[s] = hcar[s, 0:1, :]

    if pipelined:
        x1_buf = scratch[10]

        @pl.when(n == 0)
        def _first_step():
            start_weight_copies()
            x1_buf[...] = mixer()
            finish_weight_copies()

        @pl.when(jnp.logical_and(n > 0, live))
        def _steady_step():
            z = in_proj()
            first_gate = max(n_ff - n_gate_slabs, 0)
            hm, x2, _ = mlp_begin(x1_buf[...], first_gate)
            short_conv(z)
            for j in range(n_gate_slabs):
                gate_out = gate_proj(j)
                if first_gate + j < n_ff:
                    x2 = mlp_chunk(hm, x2, first_gate + j)
                rglru(j, *gate_out)
            mlp_finish(hm, x2, min(first_gate + n_gate_slabs, n_ff))
            x1_buf[...] = mixer_rest(z)

        @pl.when(n == n_tiles)
        def _last_step():
            mlp_finish(*mlp_begin(x1_buf[...], 0))

        pl.when(live)(store_state)
    else:
        if n_tiles == 1:
            start_weight_copies()
        else:
            @pl.when(n == 0)
            def _load_weights():
                start_weight_copies()
                finish_weight_copies()
        x1 = mixer()
        if emit_y:
            mlp_finish(*mlp_begin(x1, 0))
        finish_weight_copies()
        store_state()


def _matmul_weights(emit_y):
    return ("w_in", "w_gate") + (("w_out", "w_up", "w_down") if emit_y else ())


def _const_spec(arr):
    nd = arr.ndim
    return pl.BlockSpec(arr.shape, lambda n, _nd=nd: (0,) * _nd, pipeline_mode=pl.Buffered(1))


def _run_layer(x, conv0, h0, dw0, params, *, n_streams, tm, emit_y, final_norm):
    B, T, D = x.shape
    S, TM = n_streams, tm
    assert B % S == 0 and T % TM == 0 and TM % SUBLANES == 0
    rnn_conv_w, d_rnn = params["cw"].shape
    dw_conv_w, d_conv = params["dww"].shape
    assert d_rnn % MXU_DIM == 0 and d_conv % LANES == 0
    cpad = -(-(rnn_conv_w - 1) // SUBLANES) * SUBLANES
    vpad = -(-(dw_conv_w - 1) // SUBLANES) * SUBLANES
    conv_rows = min(256, TM)
    assert TM % conv_rows == 0 and conv_rows % (2 * SUBLANES) == 0
    d_ff = params["w_up"].shape[-1]
    ff_chunk = min(1024, d_ff)
    tps = T // TM
    n_tiles = (B // S) * tps
    pipelined = emit_y and n_tiles > 1

    def in_tile(n):
        return jnp.minimum(n, n_tiles - 1) if pipelined else n

    def out_tile(n):
        return jnp.maximum(n - 1, 0) if pipelined else n

    zero_state = conv0 is None
    if zero_state:
        assert h0 is None and dw0 is None
        conv0 = h0 = dw0 = params["cb"]

    def state_spec(arr):
        if zero_state:
            return pl.BlockSpec(memory_space=pl.ANY)
        blk = (S,) + arr.shape[1:]
        if arr.shape[0] == B:
            return pl.BlockSpec(blk, lambda n: (in_tile(n) // tps, 0, 0))
        assert arr.shape[0] == 1 and S == 1
        return pl.BlockSpec(blk, lambda n: (0, 0, 0))

    order = ["norm_mix", "w_in", "cw", "cb", "w_gate", "b_r", "b_i", "lam", "dww", "dwb", "ln_g", "ln_b",
             "onr", "onc", "w_out", "norm_mlp", "w_up", "w_down", "norm_final"]
    weights = [params[k] for k in order]
    in_specs = [pl.BlockSpec((S, TM, D), lambda n: (in_tile(n) // tps, in_tile(n) % tps, 0)),
                state_spec(conv0), state_spec(h0), state_spec(dw0)]
    in_hbm = ("w_in", "w_gate", "w_out", "w_up", "w_down")
    in_specs += [pl.BlockSpec(memory_space=pl.ANY) if k in in_hbm else _const_spec(w) for k, w in zip(order, weights)]

    f32 = jnp.float32
    out_shape = [jax.ShapeDtypeStruct((B, rnn_conv_w - 1, d_rnn), f32),
                 jax.ShapeDtypeStruct((B, 1, d_rnn), f32),
                 jax.ShapeDtypeStruct((B, dw_conv_w - 1, d_conv), f32)]
    out_specs = [pl.BlockSpec((S, rnn_conv_w - 1, d_rnn), lambda n: (in_tile(n) // tps, 0, 0)),
                 pl.BlockSpec((S, 1, d_rnn), lambda n: (in_tile(n) // tps, 0, 0)),
                 pl.BlockSpec((S, dw_conv_w - 1, d_conv), lambda n: (in_tile(n) // tps, 0, 0))]
    if emit_y:
        out_shape = [jax.ShapeDtypeStruct((B, T, D), x.dtype)] + out_shape
        out_specs = [pl.BlockSpec((S, TM, D), lambda n: (out_tile(n) // tps, out_tile(n) % tps, 0))] + out_specs

    M = S * TM
    lvl_rows = _scan_level_rows(TM)
    scratch_shapes = [
        pltpu.VMEM((S, d_rnn // LANES, cpad + TM, LANES), f32),
        pltpu.VMEM((S, d_conv // LANES, vpad + TM, LANES), f32),
        pltpu.VMEM((S, SUBLANES, d_rnn), f32),
        pltpu.VMEM((d_rnn // LANES, M, LANES), f32),
        pltpu.VMEM((d_rnn // LANES, M, LANES), f32),
        pltpu.VMEM((d_rnn // LANES, M, LANES), f32),
        pltpu.VMEM((2, S, d_rnn // LANES, max(sum(lvl_rows), SUBLANES), LANES), f32),
        pltpu.VMEM((S, d_rnn // LANES, max(sum(r + 2 * SUBLANES for r in lvl_rows), SUBLANES), LANES), f32),
        pltpu.VMEM((d_rnn // LANES, M, LANES), f32),
        pltpu.VMEM((d_conv // LANES, M, LANES), f32),
    ]
    w_names = _matmul_weights(emit_y)
    scratch_shapes += [pltpu.VMEM(params[k].shape, params[k].dtype) for k in w_names]
    scratch_shapes.append(pltpu.SemaphoreType.DMA((len(w_names),)))
    if pipelined:
        scratch_shapes.append(pltpu.VMEM((M, D), f32))
    kern = functools.partial(_layer_kernel, n_streams=S, tm=TM, n_tiles=n_tiles, tiles_per_stream=tps,
                             emit_y=emit_y, final_norm=final_norm, pipelined=pipelined,
                             rnn_conv_w=rnn_conv_w, dw_conv_w=dw_conv_w, ff_chunk=ff_chunk, conv_rows=conv_rows,
                             zero_state=zero_state)
    return pl.pallas_call(
        kern,
        grid=(n_tiles + (1 if pipelined else 0),),
        in_specs=in_specs,
        out_specs=out_specs,
        out_shape=out_shape,
        scratch_shapes=scratch_shapes,
        compiler_params=pltpu.CompilerParams(
            dimension_semantics=("arbitrary",),
            vmem_limit_bytes=VMEM_LIMIT_BYTES),
        name=f"hybrid_layer_s{S}_t{TM}",
    )(x, conv0, h0, dw0, *weights)


def _block_diag_gates(w_r, w_i):
    n_heads, hd, _ = w_r.shape
    per = MXU_DIM // hd
    n_slab = n_heads // per
    on_diag = jnp.eye(per, dtype=bool)[None, :, None, :, None]

    def slabs(w):
        w = w.reshape(n_slab, per, hd, hd)
        bd = jnp.where(on_diag, w[:, :, :, None, :], 0.0)
        return bd.reshape(n_slab, MXU_DIM, MXU_DIM)

    return jnp.concatenate([slabs(w_r), slabs(w_i)], axis=-1).astype(jnp.bfloat16)


def kernel(x_prompt, x_sample, state_rglru_conv, state_rglru_h, state_dwconv, meta_tokens, norm_mix, w_in,
           rnn_conv_w, rnn_conv_b, w_gate_r, b_gate_r, w_gate_i, b_gate_i, rglru_lambda, dw_w, dw_b, ln_conv_g,
           ln_conv_b, out_norm_rnn, out_norm_conv, w_out, norm_mlp, w_up, w_down, norm_final):
    depth = norm_mix.shape[0]
    bf16 = jnp.bfloat16

    xm = meta_tokens[None].astype(x_prompt.dtype)
    xp, xs = x_prompt, x_sample
    outs_p, outs_s = [], []
    for l in range(depth):
        last = l == depth - 1
        params = dict(
            norm_mix=norm_mix[l][None], w_in=w_in[l].astype(bf16), cw=rnn_conv_w[l], cb=rnn_conv_b[l][None],
            w_gate=_block_diag_gates(w_gate_r[l], w_gate_i[l]), b_r=b_gate_r[l][None], b_i=b_gate_i[l][None],
            lam=rglru_lambda[l][None], dww=dw_w[l], dwb=dw_b[l][None], ln_g=ln_conv_g[l][None],
            ln_b=ln_conv_b[l][None], onr=out_norm_rnn[l][None], onc=out_norm_conv[l][None],
            w_out=w_out[l].astype(bf16), norm_mlp=norm_mlp[l][None], w_up=w_up[l].astype(bf16),
            w_down=w_down[l].astype(bf16), norm_final=norm_final[None])
        res_m = _run_layer(xm, None, None, None, params, n_streams=1, tm=xm.shape[1], emit_y=not last,
                           final_norm=False)
        if not last:
            xm, res_m = res_m[0], res_m[1:]
        res_p = _run_layer(xp, *res_m, params, n_streams=1, tm=min(512, xp.shape[1]), emit_y=True,
                           final_norm=last)
        res_s = _run_layer(xs, state_rglru_conv[l], state_rglru_h[l][:, None], state_dwconv[l], params,
                           n_streams=xs.shape[0], tm=xs.shape[1], emit_y=True, final_norm=last)
        xp, xs = res_p[0], res_s[0]
        outs_p.append(res_p[1:])
        outs_s.append(res_s[1:])

    def stack(outs, i):
        return outs[0][i][None] if depth == 1 else jnp.stack([o[i] for o in outs])

    return (xp, xs,
            stack(outs_p, 0), stack(outs_p, 1)[:, :, 0], stack(outs_p, 2),
            stack(outs_s, 0), stack(outs_s, 1)[:, :, 0], stack(outs_s, 2))
```

```python
import functools
import math

import jax
import jax.numpy as jnp
from jax import lax
from jax.experimental import pallas as pl
from jax.experimental.pallas import tpu as pltpu

EPS = 1e-6
RGLRU_C = 8.0
SUBLANES = 8
LANES = 128
MXU_DIM = 256
VMEM_LIMIT_BYTES = 60000 * 1024


def _rms(x, g):
    ms = jnp.mean(x * x, axis=-1, keepdims=True)
    return x * lax.rsqrt(ms + EPS) * g


def _sigmoid(x):
    return 0.5 * jnp.tanh(0.5 * x) + 0.5


def _silu(x):
    h = 0.5 * x
    return h * (jnp.tanh(h) + 1.0)


def _gelu_tanh(x):
    c = math.sqrt(2.0 / math.pi)
    inner = x * (c + (0.044715 * c) * (x * x))
    return (0.5 * x) * (1.0 + jnp.tanh(inner))


def _bf16_dot(a, b):
    return jnp.dot(a.astype(jnp.bfloat16), b, preferred_element_type=jnp.float32)


def _to_slabs(dst, idx, x):
    for c in range(x.shape[-1] // LANES):
        dst[idx + (c,)] = x[:, c * LANES:(c + 1) * LANES]


def _from_slabs(buf):
    return jnp.concatenate([buf[c] for c in range(buf.shape[0])], axis=-1)


def _strided_causal_conv(win, s, w_ref, b_ref, out, *, taps, lead, tm, rows):
    half = rows // 2
    for c in range(win.shape[1]):
        lanes = slice(c * LANES, (c + 1) * LANES)
        for t0 in [i * rows + par for i in range(tm // rows) for par in range(2)]:
            acc = win[s, c, pl.ds(lead + t0, half, stride=2), :] * w_ref[0:1, lanes]
            for k in range(1, taps):
                acc = acc + win[s, c, pl.ds(lead + t0 + k, half, stride=2), :] * w_ref[k:k + 1, lanes]
            out[c, pl.ds(s * tm + t0, half, stride=2), :] = acc + b_ref[:, lanes]


def _scan_radices(n_rows):
    n_groups, radices = n_rows // SUBLANES, []
    while n_groups > 1:
        assert n_groups % 2 == 0
        r = 4 if n_groups % 4 == 0 else 2
        radices.append(r)
        n_groups //= r
    return radices


def _scan_level_rows(n_rows):
    rows, out = n_rows, []
    for r in _scan_radices(n_rows)[:-1]:
        rows //= r
        out.append(rows)
    return out


def _linear_scan_slab(a_ref, b_ref, h_ref, ab_lvl, hin_lvl, base, n_rows, carry):
    radices = _scan_radices(n_rows)
    lvl_rows = _scan_level_rows(n_rows)
    ab_off = [sum(lvl_rows[:i]) for i in range(len(lvl_rows))]
    hin_off = [sum(r + 2 * SUBLANES for r in lvl_rows[:i]) for i in range(len(lvl_rows))]

    partial = []
    rows = n_rows
    for lvl, r in enumerate(radices):
        g = rows // r
        src, off = ((a_ref, b_ref), base) if lvl == 0 else ((ab_lvl.at[0], ab_lvl.at[1]), ab_off[lvl - 1])
        ld = lambda kind, k: src[kind][pl.ds(off + k, g, stride=r), :]
        pa, pb = ld(0, 0), ld(1, 0)
        maps = [(pa, pb)]
        for k in range(1, r):
            ak, bk = ld(0, k), ld(1, k)
            pa, pb = ak * pa, ak * pb + bk
            maps.append((pa, pb))
        partial.append(maps)
        rows = g
        if lvl + 1 < len(radices):
            ab_lvl[0, ab_off[lvl]:ab_off[lvl] + g, :] = pa
            ab_lvl[1, ab_off[lvl]:ab_off[lvl] + g, :] = pb
    if not radices:
        pa, pb = a_ref[base:base + SUBLANES, :], b_ref[base:base + SUBLANES, :]

    row = lax.broadcasted_iota(jnp.int32, (SUBLANES, LANES), 0)
    for sh in (1, 2, 4):
        keep = row >= sh
        pa_s = jnp.where(keep, pltpu.roll(pa, sh, 0), 1.0)
        pb_s = jnp.where(keep, pltpu.roll(pb, sh, 0), 0.0)
        pa, pb = pa * pa_s, pa * pb_s + pb
    h_top = pa * carry + pb
    new_carry = jnp.broadcast_to(h_top[SUBLANES - 1:SUBLANES, :], (SUBLANES, LANES))
    if not radices:
        h_ref[base:base + SUBLANES, :] = h_top
        return new_carry
    h_prev = jnp.where(row == 0, carry, pltpu.roll(h_top, 1, 0))

    for lvl in reversed(range(len(radices))):
        r = radices[lvl]
        g = partial[lvl][0][0].shape[0]
        if lvl < len(radices) - 1:
            o = hin_off[lvl] + SUBLANES
            h_prev = hin_lvl[o:o + g, :]
        for k, (pa, pb) in enumerate(partial[lvl]):
            hk = pa * h_prev + pb
            if lvl == 0:
                h_ref[pl.ds(base + k, g, stride=r), :] = hk
            else:
                o = hin_off[lvl - 1] + SUBLANES
                hin_lvl[pl.ds(o + 1 + k, g, stride=r), :] = hk
        if lvl > 0:
            o = hin_off[lvl - 1] + SUBLANES
            hin_lvl[o:o + 1, :] = carry[0:1, :]
    return new_carry


def _layer_kernel(x_ref, conv0_ref, h0_ref, dw0_ref,
                  norm_mix_ref, w_in_ref, cw_ref, cb_ref, w_gate_ref, b_r_ref, b_i_ref, lam_ref,
                  dww_ref, dwb_ref, ln_g_ref, ln_b_ref, onr_ref, onc_ref,
                  w_out_ref, norm_mlp_ref, w_up_ref, w_down_ref, norm_final_ref,
                  *rest, n_streams, tm, n_tiles, tiles_per_stream, emit_y, final_norm, pipelined,
                  rnn_conv_w, dw_conv_w, ff_chunk, conv_rows, zero_state):
    if emit_y:
        y_ref, conv_out_ref, h_out_ref, dw_out_ref = rest[:4]
        scratch = rest[4:]
    else:
        conv_out_ref, h_out_ref, dw_out_ref = rest[:3]
        scratch = rest[3:]
    cbuf, vbuf, hcar, a_buf, b_buf, h_buf, ab_lvl, hin_lvl, xc_buf, vc_buf = scratch[:10]

    w_hbm = dict(w_in=w_in_ref, w_gate=w_gate_ref, w_out=w_out_ref, w_up=w_up_ref, w_down=w_down_ref)
    w_names = _matmul_weights(emit_y)
    w_vmem = dict(zip(w_names, scratch[10:10 + len(w_names)]))
    w_sem = scratch[10 + len(w_names)]
    scratch = scratch[:10] + scratch[11 + len(w_names):]
    in_flight = set()

    def weight_copy(name):
        return pltpu.make_async_copy(w_hbm[name], w_vmem[name], w_sem.at[w_names.index(name)])

    def start_weight_copies():
        for name in w_names:
            weight_copy(name).start()
            in_flight.add(name)

    def finish_weight_copies():
        for name in w_names:
            weight(name)

    def weight(name):
        if name in in_flight:
            weight_copy(name).wait()
            in_flight.discard(name)
        return w_vmem[name]

    S, TM = n_streams, tm
    M = S * TM
    d_rnn = cw_ref.shape[-1]
    d_conv = dww_ref.shape[-1]
    cpad = cbuf.shape[2] - TM
    vpad = vbuf.shape[2] - TM
    c_hist = rnn_conv_w - 1
    v_hist = dw_conv_w - 1
    n = pl.program_id(0)
    tile = jnp.minimum(n, n_tiles - 1)
    live = n < n_tiles

    @pl.when(jnp.logical_and(lax.rem(tile, tiles_per_stream) == 0, live))
    def _load_state():
        if zero_state:
            cbuf[:, :, 0:cpad, :] = jnp.zeros((S, d_rnn // LANES, cpad, LANES), jnp.float32)
            vbuf[:, :, 0:vpad, :] = jnp.zeros((S, d_conv // LANES, vpad, LANES), jnp.float32)
            hcar[...] = jnp.zeros(hcar.shape, jnp.float32)
            return
        cbuf[:, :, 0:cpad - c_hist, :] = jnp.zeros((S, d_rnn // LANES, cpad - c_hist, LANES), jnp.float32)
        vbuf[:, :, 0:vpad - v_hist, :] = jnp.zeros((S, d_conv // LANES, vpad - v_hist, LANES), jnp.float32)
        for c in range(d_rnn // LANES):
            cbuf[:, c, cpad - c_hist:cpad, :] = conv0_ref[:, :, c * LANES:(c + 1) * LANES]
        for c in range(d_conv // LANES):
            vbuf[:, c, vpad - v_hist:vpad, :] = dw0_ref[:, :, c * LANES:(c + 1) * LANES]
        hcar[...] = jnp.broadcast_to(h0_ref[...], hcar.shape)

    def in_proj():
        x = x_ref[...].reshape(M, x_ref.shape[-1])
        hn = _rms(x, norm_mix_ref[...])
        return _bf16_dot(hn, weight("w_in")[...])

    n_gate_slabs = d_rnn // MXU_DIM
    lanes_per_gate_slab = MXU_DIM // LANES

    def short_conv(z):
        xr = z[:, 0:d_rnn]
        for s in range(S):
            _to_slabs(cbuf.at[:, :, cpad:cpad + TM, :], (s,), xr[s * TM:(s + 1) * TM, :])
            _strided_causal_conv(cbuf, s, cw_ref, cb_ref, xc_buf, taps=rnn_conv_w, lead=cpad - c_hist, tm=TM,
                                 rows=conv_rows)
            for c in range(d_rnn // LANES):
                cbuf[s, c, 0:cpad, :] = cbuf[s, c, TM:TM + cpad, :]

    def gate_proj(j):
        slabs = range(j * lanes_per_gate_slab, (j + 1) * lanes_per_gate_slab)
        xc = jnp.concatenate([xc_buf[c] for c in slabs], axis=-1)
        return xc, _bf16_dot(xc, weight("w_gate")[j])

    def rglru(j, xc, ri):
        lanes = slice(j * MXU_DIM, (j + 1) * MXU_DIM)
        r = _sigmoid(ri[:, :MXU_DIM] + b_r_ref[:, lanes])
        ig = _sigmoid(ri[:, MXU_DIM:] + b_i_ref[:, lanes])
        nl = -lam_ref[:, lanes]
        softplus_nl = jnp.maximum(nl, 0.0) + jnp.log1p(jnp.exp(-jnp.abs(nl)))
        a = jnp.exp((-RGLRU_C * r) * softplus_nl)
        b = jnp.sqrt(jnp.maximum(1.0 - a * a, 0.0)) * ig * xc
        for k in range(lanes_per_gate_slab):
            c = j * lanes_per_gate_slab + k
            a_buf[c] = a[:, k * LANES:(k + 1) * LANES]
            b_buf[c] = b[:, k * LANES:(k + 1) * LANES]
            for s in range(S):
                hcar[s, :, c * LANES:(c + 1) * LANES] = _linear_scan_slab(
                    a_buf.at[c], b_buf.at[c], h_buf.at[c], ab_lvl.at[:, s, c], hin_lvl.at[s, c], s * TM, TM,
                    hcar[s, :, c * LANES:(c + 1) * LANES])

    def mixer_rest(z):
        x = x_ref[...].reshape(M, x_ref.shape[-1])
        gate = z[:, d_rnn:2 * d_rnn]
        glu_v = z[:, 2 * d_rnn:2 * d_rnn + d_conv]
        glu_g = z[:, 2 * d_rnn + d_conv:]

        v = glu_v * _sigmoid(glu_g)
        for s in range(S):
            _to_slabs(vbuf.at[:, :, vpad:vpad + TM, :], (s,), v[s * TM:(s + 1) * TM, :])
            _strided_causal_conv(vbuf, s, dww_ref, dwb_ref, vc_buf, taps=dw_conv_w, lead=vpad - v_hist, tm=TM,
                                 rows=conv_rows)
            for c in range(d_conv // LANES):
                vbuf[s, c, 0:vpad, :] = vbuf[s, c, TM:TM + vpad, :]
        if not emit_y:
            return None

        y_rnn = _from_slabs(h_buf) * _gelu_tanh(gate)
        vc = _from_slabs(vc_buf)
        mu = jnp.mean(vc, axis=-1, keepdims=True)
        vcc = vc - mu
        ln = vcc * lax.rsqrt(jnp.mean(vcc * vcc, axis=-1, keepdims=True) + EPS) * ln_g_ref[...] + ln_b_ref[...]
        y_conv = _silu(ln)
        mix = jnp.concatenate([_rms(y_rnn, onr_ref[...]), _rms(y_conv, onc_ref[...])], axis=-1)
        return x + _bf16_dot(mix, weight("w_out")[...])

    def mixer():
        z = in_proj()
        short_conv(z)
        for j in range(n_gate_slabs):
            rglru(j, *gate_proj(j))
        return mixer_rest(z)

    n_ff = w_up_ref.shape[-1] // ff_chunk

    def mlp_chunk(hm, x2, c):
        hc = jnp.dot(hm, weight("w_up")[:, c * ff_chunk:(c + 1) * ff_chunk], preferred_element_type=jnp.float32)
        hc = jnp.square(jnp.maximum(hc, 0.0))
        return x2 + _bf16_dot(hc, weight("w_down")[c * ff_chunk:(c + 1) * ff_chunk, :])

    def mlp_begin(x1, n_chunks):
        hm = _rms(x1, norm_mlp_ref[...]).astype(jnp.bfloat16)
        x2 = x1
        for c in range(n_chunks):
            x2 = mlp_chunk(hm, x2, c)
        return hm, x2, n_chunks

    def mlp_finish(hm, x2, n_done):
        for c in range(n_done, n_ff):
            x2 = mlp_chunk(hm, x2, c)
        if final_norm:
            x2 = _rms(x2, norm_final_ref[...])
        y_ref[...] = x2.reshape(y_ref.shape)

    def store_state():
        for s in range(S):
            for c in range(d_rnn // LANES):
                conv_out_ref[s, :, c * LANES:(c + 1) * LANES] = cbuf[s, c, cpad - c_hist:cpad, :]
            for c in range(d_conv // LANES):
                dw_out_ref[s, :, c * LANES:(c + 1) * LANES] = vbuf[s, c, vpad - v_hist:vpad, :]
            h_out_ref[s] = hcar[s, 0:1, :]

    if pipelined:
        x1_buf = scratch[10]

        @pl.when(n == 0)
        def _first_step():
            start_weight_copies()
            x1_buf[...] = mixer()
            finish_weight_copies()

        @pl.when(jnp.logical_and(n > 0, live))
        def _steady_step():
            z = in_proj()
            first_gate = max(n_ff - n_gate_slabs, 0)
            short_conv(z)
            hm, x2, _ = mlp_begin(x1_buf[...], first_gate)
            for j in range(n_gate_slabs):
                gate_out = gate_proj(j)
                if first_gate + j < n_ff:
                    x2 = mlp_chunk(hm, x2, first_gate + j)
                rglru(j, *gate_out)
            mlp_finish(hm, x2, min(first_gate + n_gate_slabs, n_ff))
            x1_buf[...] = mixer_rest(z)

        @pl.when(n == n_tiles)
        def _last_step():
            mlp_finish(*mlp_begin(x1_buf[...], 0))

        pl.when(live)(store_state)
    else:
        if n_tiles == 1:
            start_weight_copies()
        else:
            @pl.when(n == 0)
            def _load_weights():
                start_weight_copies()
                finish_weight_copies()
        x1 = mixer()
        if emit_y:
            mlp_finish(*mlp_begin(x1, 0))
        finish_weight_copies()
        store_state()


def _matmul_weights(emit_y):
    return ("w_in", "w_gate") + (("w_out", "w_up", "w_down") if emit_y else ())


def _const_spec(arr):
    nd = arr.ndim
    return pl.BlockSpec(arr.shape, lambda n, _nd=nd: (0,) * _nd, pipeline_mode=pl.Buffered(1))


def _run_layer(x, conv0, h0, dw0, params, *, n_streams, tm, emit_y, final_norm):
    B, T, D = x.shape
    S, TM = n_streams, tm
    assert B % S == 0 and T % TM == 0 and TM % SUBLANES == 0
    rnn_conv_w, d_rnn = params["cw"].shape
    dw_conv_w, d_conv = params["dww"].shape
    assert d_rnn % MXU_DIM == 0 and d_conv % LANES == 0
    cpad = -(-(rnn_conv_w - 1) // SUBLANES) * SUBLANES
    vpad = -(-(dw_conv_w - 1) // SUBLANES) * SUBLANES
    conv_rows = min(256, TM)
    assert TM % conv_rows == 0 and conv_rows % (2 * SUBLANES) == 0
    d_ff = params["w_up"].shape[-1]
    ff_chunk = min(1024, d_ff)
    tps = T // TM
    n_tiles = (B // S) * tps
    pipelined = emit_y and n_tiles > 1

    def in_tile(n):
        return jnp.minimum(n, n_tiles - 1) if pipelined else n

    def out_tile(n):
        return jnp.maximum(n - 1, 0) if pipelined else n

    zero_state = conv0 is None
    if zero_state:
        assert h0 is None and dw0 is None
        conv0 = h0 = dw0 = params["cb"]

    def state_spec(arr):
        if zero_state:
            return pl.BlockSpec(memory_space=pl.ANY)
        blk = (S,) + arr.shape[1:]
        if arr.shape[0] == B:
            return pl.BlockSpec(blk, lambda n: (in_tile(n) // tps, 0, 0))
        assert arr.shape[0] == 1 and S == 1
        return pl.BlockSpec(blk, lambda n: (0, 0, 0))

    order = ["norm_mix", "w_in", "cw", "cb", "w_gate", "b_r", "b_i", "lam", "dww", "dwb", "ln_g", "ln_b",
             "onr", "onc", "w_out", "norm_mlp", "w_up", "w_down", "norm_final"]
    weights = [params[k] for k in order]
    in_specs = [pl.BlockSpec((S, TM, D), lambda n: (in_tile(n) // tps, in_tile(n) % tps, 0)),
                state_spec(conv0), state_spec(h0), state_spec(dw0)]
    in_hbm = ("w_in", "w_gate", "w_out", "w_up", "w_down")
    in_specs += [pl.BlockSpec(memory_space=pl.ANY) if k in in_hbm else _const_spec(w) for k, w in zip(order, weights)]

    f32 = jnp.float32
    out_shape = [jax.ShapeDtypeStruct((B, rnn_conv_w - 1, d_rnn), f32),
                 jax.ShapeDtypeStruct((B, 1, d_rnn), f32),
                 jax.ShapeDtypeStruct((B, dw_conv_w - 1, d_conv), f32)]
    out_specs = [pl.BlockSpec((S, rnn_conv_w - 1, d_rnn), lambda n: (in_tile(n) // tps, 0, 0)),
                 pl.BlockSpec((S, 1, d_rnn), lambda n: (in_tile(n) // tps, 0, 0)),
                 pl.BlockSpec((S, dw_conv_w - 1, d_conv), lambda n: (in_tile(n) // tps, 0, 0))]
    if emit_y:
        out_shape = [jax.ShapeDtypeStruct((B, T, D), x.dtype)] + out_shape
        out_specs = [pl.BlockSpec((S, TM, D), lambda n: (out_tile(n) // tps, out_tile(n) % tps, 0))] + out_specs

    M = S * TM
    lvl_rows = _scan_level_rows(TM)
    scratch_shapes = [
        pltpu.VMEM((S, d_rnn // LANES, cpad + TM, LANES), f32),
        pltpu.VMEM((S, d_conv // LANES, vpad + TM, LANES), f32),
        pltpu.VMEM((S, SUBLANES, d_rnn), f32),
        pltpu.VMEM((d_rnn // LANES, M, LANES), f32),
        pltpu.VMEM((d_rnn // LANES, M, LANES), f32),
        pltpu.VMEM((d_rnn // LANES, M, LANES), f32),
        pltpu.VMEM((2, S, d_rnn // LANES, max(sum(lvl_rows), SUBLANES), LANES), f32),
        pltpu.VMEM((S, d_rnn // LANES, max(sum(r + 2 * SUBLANES for r in lvl_rows), SUBLANES), LANES), f32),
        pltpu.VMEM((d_rnn // LANES, M, LANES), f32),
        pltpu.VMEM((d_conv // LANES, M, LANES), f32),
    ]
    w_names = _matmul_weights(emit_y)
    scratch_shapes += [pltpu.VMEM(params[k].shape, params[k].dtype) for k in w_names]
    scratch_shapes.append(pltpu.SemaphoreType.DMA((len(w_names),)))
    if pipelined:
        scratch_shapes.append(pltpu.VMEM((M, D), f32))
    kern = functools.partial(_layer_kernel, n_streams=S, tm=TM, n_tiles=n_tiles, tiles_per_stream=tps,
                             emit_y=emit_y, final_norm=final_norm, pipelined=pipelined,
                             rnn_conv_w=rnn_conv_w, dw_conv_w=dw_conv_w, ff_chunk=ff_chunk, conv_rows=conv_rows,
                             zero_state=zero_state)
    return pl.pallas_call(
        kern,
        grid=(n_tiles + (1 if pipelined else 0),),
        in_specs=in_specs,
        out_specs=out_specs,
        out_shape=out_shape,
        scratch_shapes=scratch_shapes,
        compiler_params=pltpu.CompilerParams(
            dimension_semantics=("arbitrary",),
            vmem_limit_bytes=VMEM_LIMIT_BYTES),
        name=f"hybrid_layer_s{S}_t{TM}",
    )(x, conv0, h0, dw0, *weights)


def _block_diag_gates(w_r, w_i):
    n_heads, hd, _ = w_r.shape
    per = MXU_DIM // hd
    n_slab = n_heads // per
    on_diag = jnp.eye(per, dtype=bool)[None, :, None, :, None]

    def slabs(w):
        w = w.reshape(n_slab, per, hd, hd)
        bd = jnp.where(on_diag, w[:, :, :, None, :], 0.0)
        return bd.reshape(n_slab, MXU_DIM, MXU_DIM)

    return jnp.concatenate([slabs(w_r), slabs(w_i)], axis=-1).astype(jnp.bfloat16)


def kernel(x_prompt, x_sample, state_rglru_conv, state_rglru_h, state_dwconv, meta_tokens, norm_mix, w_in,
           rnn_conv_w, rnn_conv_b, w_gate_r, b_gate_r, w_gate_i, b_gate_i, rglru_lambda, dw_w, dw_b, ln_conv_g,
           ln_conv_b, out_norm_rnn, out_norm_conv, w_out, norm_mlp, w_up, w_down, norm_final):
    depth = norm_mix.shape[0]
    bf16 = jnp.bfloat16

    xm = meta_tokens[None].astype(x_prompt.dtype)
    xp, xs = x_prompt, x_sample
    outs_p, outs_s = [], []
    for l in range(depth):
        last = l == depth - 1
        params = dict(
            norm_mix=norm_mix[l][None], w_in=w_in[l].astype(bf16), cw=rnn_conv_w[l], cb=rnn_conv_b[l][None],
            w_gate=_block_diag_gates(w_gate_r[l], w_gate_i[l]), b_r=b_gate_r[l][None], b_i=b_gate_i[l][None],
            lam=rglru_lambda[l][None], dww=dw_w[l], dwb=dw_b[l][None], ln_g=ln_conv_g[l][None],
            ln_b=ln_conv_b[l][None], onr=out_norm_rnn[l][None], onc=out_norm_conv[l][None],
            w_out=w_out[l].astype(bf16), norm_mlp=norm_mlp[l][None], w_up=w_up[l].astype(bf16),
            w_down=w_down[l].astype(bf16), norm_final=norm_final[None])
        res_m = _run_layer(xm, None, None, None, params, n_streams=1, tm=xm.shape[1], emit_y=not last,
                           final_norm=False)
        if not last:
            xm, res_m = res_m[0], res_m[1:]
        res_p = _run_layer(xp, *res_m, params, n_streams=1, tm=min(512, xp.shape[1]), emit_y=True,
                           final_norm=last)
        res_s = _run_layer(xs, state_rglru_conv[l], state_rglru_h[l][:, None], state_dwconv[l], params,
                           n_streams=xs.shape[0], tm=xs.shape[1], emit_y=True, final_norm=last)
        xp, xs = res_p[0], res_s[0]
        outs_p.append(res_p[1:])
        outs_s.append(res_s[1:])

    def stack(outs, i):
        return outs[0][i][None] if depth == 1 else jnp.stack([o[i] for o in outs])

    return (xp, xs,
            stack(outs_p, 0), stack(outs_p, 1)[:, :, 0], stack(outs_p, 2),
            stack(outs_s, 0), stack(outs_s, 1)[:, :, 0], stack(outs_s, 2))
```

```python
import functools
import math

import jax
import jax.numpy as jnp
from jax import lax
from jax.experimental import pallas as pl
from jax.experimental.pallas import tpu as pltpu

EPS = 1e-6
RGLRU_C = 8.0
SUBLANES = 8
LANES = 128
MXU_DIM = 256
VMEM_LIMIT_BYTES = 60000 * 1024


def _rms(x, g):
    ms = jnp.mean(x * x, axis=-1, keepdims=True)
    return x * lax.rsqrt(ms + EPS) * g


def _sigmoid(x):
    return 0.5 * jnp.tanh(0.5 * x) + 0.5


def _silu(x):
    h = 0.5 * x
    return h * (jnp.tanh(h) + 1.0)


def _gelu_tanh(x):
    c = math.sqrt(2.0 / math.pi)
    inner = x * (c + (0.044715 * c) * (x * x))
    return (0.5 * x) * (1.0 + jnp.tanh(inner))


def _bf16_dot(a, b):
    return jnp.dot(a.astype(jnp.bfloat16), b, preferred_element_type=jnp.float32)


def _to_slabs(dst, idx, x):
    for c in range(x.shape[-1] // LANES):
        dst[idx + (c,)] = x[:, c * LANES:(c + 1) * LANES]


def _from_slabs(buf):
    return jnp.concatenate([buf[c] for c in range(buf.shape[0])], axis=-1)


def _strided_causal_conv(win, s, w_ref, b_ref, out, *, taps, lead, tm, rows):
    half = rows // 2
    for c in range(win.shape[1]):
        lanes = slice(c * LANES, (c + 1) * LANES)
        for t0 in [i * rows + par for i in range(tm // rows) for par in range(2)]:
            acc = win[s, c, pl.ds(lead + t0, half, stride=2), :] * w_ref[0:1, lanes]
            for k in range(1, taps):
                acc = acc + win[s, c, pl.ds(lead + t0 + k, half, stride=2), :] * w_ref[k:k + 1, lanes]
            out[c, pl.ds(s * tm + t0, half, stride=2), :] = acc + b_ref[:, lanes]


def _scan_radices(n_rows):
    n_groups, radices = n_rows // SUBLANES, []
    while n_groups > 1:
        assert n_groups % 2 == 0
        r = 4 if n_groups % 4 == 0 else 2
        radices.append(r)
        n_groups //= r
    return radices


def _scan_level_rows(n_rows):
    rows, out = n_rows, []
    for r in _scan_radices(n_rows)[:-1]:
        rows //= r
        out.append(rows)
    return out


def _linear_scan_slab(a_ref, b_ref, h_ref, ab_lvl, hin_lvl, base, n_rows, carry):
    radices = _scan_radices(n_rows)
    lvl_rows = _scan_level_rows(n_rows)
    ab_off = [sum(lvl_rows[:i]) for i in range(len(lvl_rows))]
    hin_off = [sum(r + 2 * SUBLANES for r in lvl_rows[:i]) for i in range(len(lvl_rows))]

    partial = []
    rows = n_rows
    for lvl, r in enumerate(radices):
        g = rows // r
        src, off = ((a_ref, b_ref), base) if lvl == 0 else ((ab_lvl.at[0], ab_lvl.at[1]), ab_off[lvl - 1])
        ld = lambda kind, k: src[kind][pl.ds(off + k, g, stride=r), :]
        pa, pb = ld(0, 0), ld(1, 0)
        maps = [(pa, pb)]
        for k in range(1, r):
            ak, bk = ld(0, k), ld(1, k)
            pa, pb = ak * pa, ak * pb + bk
            maps.append((pa, pb))
        partial.append(maps)
        rows = g
        if lvl + 1 < len(radices):
            ab_lvl[0, ab_off[lvl]:ab_off[lvl] + g, :] = pa
            ab_lvl[1, ab_off[lvl]:ab_off[lvl] + g, :] = pb
    if not radices:
        pa, pb = a_ref[base:base + SUBLANES, :], b_ref[base:base + SUBLANES, :]

    row = lax.broadcasted_iota(jnp.int32, (SUBLANES, LANES), 0)
    for sh in (1, 2, 4):
        keep = row >= sh
        pa_s = jnp.where(keep, pltpu.roll(pa, sh, 0), 1.0)
        pb_s = jnp.where(keep, pltpu.roll(pb, sh, 0), 0.0)
        pa, pb = pa * pa_s, pa * pb_s + pb
    h_top = pa * carry + pb
    new_carry = jnp.broadcast_to(h_top[SUBLANES - 1:SUBLANES, :], (SUBLANES, LANES))
    if not radices:
        h_ref[base:base + SUBLANES, :] = h_top
        return new_carry
    h_prev = jnp.where(row == 0, carry, pltpu.roll(h_top, 1, 0))

    for lvl in reversed(range(len(radices))):
        r = radices[lvl]
        g = partial[lvl][0][0].shape[0]
        if lvl < len(radices) - 1:
            o = hin_off[lvl] + SUBLANES
            h_prev = hin_lvl[o:o + g, :]
        for k, (pa, pb) in enumerate(partial[lvl]):
            hk = pa * h_prev + pb
            if lvl == 0:
                h_ref[pl.ds(base + k, g, stride=r), :] = hk
            else:
                o = hin_off[lvl - 1] + SUBLANES
                hin_lvl[pl.ds(o + 1 + k, g, stride=r), :] = hk
        if lvl > 0:
            o = hin_off[lvl - 1] + SUBLANES
            hin_lvl[o:o + 1, :] = carry[0:1, :]
    return new_carry


def _layer_kernel(x_ref, conv0_ref, h0_ref, dw0_ref,
                  norm_mix_ref, w_in_ref, cw_ref, cb_ref, w_gate_ref, b_r_ref, b_i_ref, lam_ref,
                  dww_ref, dwb_ref, ln_g_ref, ln_b_ref, onr_ref, onc_ref,
                  w_out_ref, norm_mlp_ref, w_up_ref, w_down_ref, norm_final_ref,
                  *rest, n_streams, tm, n_tiles, tiles_per_stream, emit_y, final_norm, pipelined,
                  rnn_conv_w, dw_conv_w, ff_chunk, conv_rows, zero_state):
    if emit_y:
        y_ref, conv_out_ref, h_out_ref, dw_out_ref = rest[:4]
        scratch = rest[4:]
    else:
        conv_out_ref, h_out_ref, dw_out_ref = rest[:3]
        scratch = rest[3:]
    cbuf, vbuf, hcar, a_buf, b_buf, h_buf, ab_lvl, hin_lvl, xc_buf, vc_buf = scratch[:10]

    w_hbm = dict(w_in=w_in_ref, w_gate=w_gate_ref, w_out=w_out_ref, w_up=w_up_ref, w_down=w_down_ref)
    w_names = _matmul_weights(emit_y)
    w_vmem = dict(zip(w_names, scratch[10:10 + len(w_names)]))
    w_sem = scratch[10 + len(w_names)]
    scratch = scratch[:10] + scratch[11 + len(w_names):]
    in_flight = set()

    def weight_copy(name):
        return pltpu.make_async_copy(w_hbm[name], w_vmem[name], w_sem.at[w_names.index(name)])

    def start_weight_copies():
        for name in w_names:
            weight_copy(name).start()
            in_flight.add(name)

    def finish_weight_copies():
        for name in w_names:
            weight(name)

    def weight(name):
        if name in in_flight:
            weight_copy(name).wait()
            in_flight.discard(name)
        return w_vmem[name]

    S, TM = n_streams, tm
    M = S * TM
    d_rnn = cw_ref.shape[-1]
    d_conv = dww_ref.shape[-1]
    cpad = cbuf.shape[2] - TM
    vpad = vbuf.shape[2] - TM
    c_hist = rnn_conv_w - 1
    v_hist = dw_conv_w - 1
    n = pl.program_id(0)
    tile = jnp.minimum(n, n_tiles - 1)
    live = n < n_tiles

    @pl.when(jnp.logical_and(lax.rem(tile, tiles_per_stream) == 0, live))
    def _load_state():
        if zero_state:
            cbuf[:, :, 0:cpad, :] = jnp.zeros((S, d_rnn // LANES, cpad, LANES), jnp.float32)
            vbuf[:, :, 0:vpad, :] = jnp.zeros((S, d_conv // LANES, vpad, LANES), jnp.float32)
            hcar[...] = jnp.zeros(hcar.shape, jnp.float32)
            return
        cbuf[:, :, 0:cpad - c_hist, :] = jnp.zeros((S, d_rnn // LANES, cpad - c_hist, LANES), jnp.float32)
        vbuf[:, :, 0:vpad - v_hist, :] = jnp.zeros((S, d_conv // LANES, vpad - v_hist, LANES), jnp.float32)
        for c in range(d_rnn // LANES):
            cbuf[:, c, cpad - c_hist:cpad, :] = conv0_ref[:, :, c * LANES:(c + 1) * LANES]
        for c in range(d_conv // LANES):
            vbuf[:, c, vpad - v_hist:vpad, :] = dw0_ref[:, :, c * LANES:(c + 1) * LANES]
        hcar[...] = jnp.broadcast_to(h0_ref[...], hcar.shape)

    def in_proj():
        x = x_ref[...].reshape(M, x_ref.shape[-1])
        hn = _rms(x, norm_mix_ref[...])
        return _bf16_dot(hn, weight("w_in")[...])

    n_gate_slabs = d_rnn // MXU_DIM
    lanes_per_gate_slab = MXU_DIM // LANES

    def short_conv(z):
        xr = z[:, 0:d_rnn]
        for s in range(S):
            _to_slabs(cbuf.at[:, :, cpad:cpad + TM, :], (s,), xr[s * TM:(s + 1) * TM, :])
            _strided_causal_conv(cbuf, s, cw_ref, cb_ref, xc_buf, taps=rnn_conv_w, lead=cpad - c_hist, tm=TM,
                                 rows=conv_rows)
            for c in range(d_rnn // LANES):
                cbuf[s, c, 0:cpad, :] = cbuf[s, c, TM:TM + cpad, :]

    def gate_proj(j):
        slabs = range(j * lanes_per_gate_slab, (j + 1) * lanes_per_gate_slab)
        xc = jnp.concatenate([xc_buf[c] for c in slabs], axis=-1)
        return xc, _bf16_dot(xc, weight("w_gate")[j])

    def rglru(j, xc, ri):
        lanes = slice(j * MXU_DIM, (j + 1) * MXU_DIM)
        r = _sigmoid(ri[:, :MXU_DIM] + b_r_ref[:, lanes])
        ig = _sigmoid(ri[:, MXU_DIM:] + b_i_ref[:, lanes])
        nl = -lam_ref[:, lanes]
        softplus_nl = jnp.maximum(nl, 0.0) + jnp.log1p(jnp.exp(-jnp.abs(nl)))
        a = jnp.exp((-RGLRU_C * r) * softplus_nl)
        b = jnp.sqrt(jnp.maximum(1.0 - a * a, 0.0)) * ig * xc
        for k in range(lanes_per_gate_slab):
            c = j * lanes_per_gate_slab + k
            a_buf[c] = a[:, k * LANES:(k + 1) * LANES]
            b_buf[c] = b[:, k * LANES:(k + 1) * LANES]
            for s in range(S):
                hcar[s, :, c * LANES:(c + 1) * LANES] = _linear_scan_slab(
                    a_buf.at[c], b_buf.at[c], h_buf.at[c], ab_lvl.at[:, s, c], hin_lvl.at[s, c], s * TM, TM,
                    hcar[s, :, c * LANES:(c + 1) * LANES])

    def mixer_rest(z):
        x = x_ref[...].reshape(M, x_ref.shape[-1])
        gate = z[:, d_rnn:2 * d_rnn]
        glu_v = z[:, 2 * d_rnn:2 * d_rnn + d_conv]
        glu_g = z[:, 2 * d_rnn + d_conv:]

        v = glu_v * _sigmoid(glu_g)
        for s in range(S):
            _to_slabs(vbuf.at[:, :, vpad:vpad + TM, :], (s,), v[s * TM:(s + 1) * TM, :])
            _strided_causal_conv(vbuf, s, dww_ref, dwb_ref, vc_buf, taps=dw_conv_w, lead=vpad - v_hist, tm=TM,
                                 rows=conv_rows)
            for c in range(d_conv // LANES):
                vbuf[s, c, 0:vpad, :] = vbuf[s, c, TM:TM + vpad, :]
        if not emit_y:
            return None

        y_rnn = _from_slabs(h_buf) * _gelu_tanh(gate)
        vc = _from_slabs(vc_buf)
        mu = jnp.mean(vc, axis=-1, keepdims=True)
        vcc = vc - mu
        ln = vcc * lax.rsqrt(jnp.mean(vcc * vcc, axis=-1, keepdims=True) + EPS) * ln_g_ref[...] + ln_b_ref[...]
        y_conv = _silu(ln)
        mix = jnp.concatenate([_rms(y_rnn, onr_ref[...]), _rms(y_conv, onc_ref[...])], axis=-1)
        return x + _bf16_dot(mix, weight("w_out")[...])

    def mixer():
        z = in_proj()
        short_conv(z)
        for j in range(n_gate_slabs):
            rglru(j, *gate_proj(j))
        return mixer_rest(z)

    n_ff = w_up_ref.shape[-1] // ff_chunk

    def mlp_chunk(hm, x2, c):
        hc = jnp.dot(hm, weight("w_up")[:, c * ff_chunk:(c + 1) * ff_chunk], preferred_element_type=jnp.float32)
        hc = jnp.square(jnp.maximum(hc, 0.0))
        return x2 + _bf16_dot(hc, weight("w_down")[c * ff_chunk:(c + 1) * ff_chunk, :])

    def mlp_begin(x1, n_chunks):
        hm = _rms(x1, norm_mlp_ref[...]).astype(jnp.bfloat16)
        x2 = x1
        for c in range(n_chunks):
            x2 = mlp_chunk(hm, x2, c)
        return hm, x2, n_chunks

    def mlp_finish(hm, x2, n_done):
        for c in range(n_done, n_ff):
            x2 = mlp_chunk(hm, x2, c)
        if final_norm:
            x2 = _rms(x2, norm_final_ref[...])
        y_ref[...] = x2.reshape(y_ref.shape)

    def store_state():
        for s in range(S):
            for c in range(d_rnn // LANES):
                conv_out_ref[s, :, c * LANES:(c + 1) * LANES] = cbuf[s, c, cpad - c_hist:cpad, :]
            for c in range(d_conv // LANES):
                dw_out_ref[s, :, c * LANES:(c + 1) * LANES] = vbuf[s, c, vpad - v_hist:vpad, :]
            h_out_ref[s] = hcar[s, 0:1, :]

    if pipelined:
        x1_buf = scratch[10]

        @pl.when(n == 0)
        def _first_step():
            start_weight_copies()
            x1_buf[...] = mixer()
            finish_weight_copies()

        @pl.when(jnp.logical_and(n > 0, live))
        def _steady_step():
            z = in_proj()
            first_gate = max(n_ff - n_gate_slabs, 0)
            hm, x2, _ = mlp_begin(x1_buf[...], first_gate)
            short_conv(z)
            for j in range(n_gate_slabs):
                gate_out = gate_proj(j)
                if first_gate + j < n_ff:
                    x2 = mlp_chunk(hm, x2, first_gate + j)
                rglru(j, *gate_out)
            mlp_finish(hm, x2, min(first_gate + n_gate_slabs, n_ff))
            x1_buf[...] = mixer_rest(z)

        @pl.when(n == n_tiles)
        def _last_step():
            mlp_finish(*mlp_begin(x1_buf[...], 0))

        pl.when(live)(store_state)
    else:
        if n_tiles == 1:
            start_weight_copies()
        else:
            @pl.when(n == 0)
            def _load_weights():
                start_weight_copies()
                finish_weight_copies()
        x1 = mixer()
        if emit_y:
            mlp_finish(*mlp_begin(x1, 0))
        finish_weight_copies()
        store_state()


def _matmul_weights(emit_y):
    return ("w_in", "w_gate") + (("w_out", "w_up", "w_down") if emit_y else ())


def _const_spec(arr):
    nd = arr.ndim
    return pl.BlockSpec(arr.shape, lambda n, _nd=nd: (0,) * _nd, pipeline_mode=pl.Buffered(1))


def _run_layer(x, conv0, h0, dw0, params, *, n_streams, tm, emit_y, final_norm):
    B, T, D = x.shape
    S, TM = n_streams, tm
    assert B % S == 0 and T % TM == 0 and TM % SUBLANES == 0
    rnn_conv_w, d_rnn = params["cw"].shape
    dw_conv_w, d_conv = params["dww"].shape
    assert d_rnn % MXU_DIM == 0 and d_conv % LANES == 0
    cpad = -(-(rnn_conv_w - 1) // SUBLANES) * SUBLANES
    vpad = -(-(dw_conv_w - 1) // SUBLANES) * SUBLANES
    conv_rows = min(256, TM)
    assert TM % conv_rows == 0 and conv_rows % (2 * SUBLANES) == 0
    d_ff = params["w_up"].shape[-1]
    ff_chunk = min(2048, d_ff)
    tps = T // TM
    n_tiles = (B // S) * tps
    pipelined = emit_y and n_tiles > 1

    def in_tile(n):
        return jnp.minimum(n, n_tiles - 1) if pipelined else n

    def out_tile(n):
        return jnp.maximum(n - 1, 0) if pipelined else n

    zero_state = conv0 is None
    if zero_state:
        assert h0 is None and dw0 is None
        conv0 = h0 = dw0 = params["cb"]

    def state_spec(arr):
        if zero_state:
            return pl.BlockSpec(memory_space=pl.ANY)
        blk = (S,) + arr.shape[1:]
        if arr.shape[0] == B:
            return pl.BlockSpec(blk, lambda n: (in_tile(n) // tps, 0, 0))
        assert arr.shape[0] == 1 and S == 1
        return pl.BlockSpec(blk, lambda n: (0, 0, 0))

    order = ["norm_mix", "w_in", "cw", "cb", "w_gate", "b_r", "b_i", "lam", "dww", "dwb", "ln_g", "ln_b",
             "onr", "onc", "w_out", "norm_mlp", "w_up", "w_down", "norm_final"]
    weights = [params[k] for k in order]
    in_specs = [pl.BlockSpec((S, TM, D), lambda n: (in_tile(n) // tps, in_tile(n) % tps, 0)),
                state_spec(conv0), state_spec(h0), state_spec(dw0)]
    in_hbm = ("w_in", "w_gate", "w_out", "w_up", "w_down")
    in_specs += [pl.BlockSpec(memory_space=pl.ANY) if k in in_hbm else _const_spec(w) for k, w in zip(order, weights)]

    f32 = jnp.float32
    out_shape = [jax.ShapeDtypeStruct((B, rnn_conv_w - 1, d_rnn), f32),
                 jax.ShapeDtypeStruct((B, 1, d_rnn), f32),
                 jax.ShapeDtypeStruct((B, dw_conv_w - 1, d_conv), f32)]
    out_specs = [pl.BlockSpec((S, rnn_conv_w - 1, d_rnn), lambda n: (in_tile(n) // tps, 0, 0)),
                 pl.BlockSpec((S, 1, d_rnn), lambda n: (in_tile(n) // tps, 0, 0)),
                 pl.BlockSpec((S, dw_conv_w - 1, d_conv), lambda n: (in_tile(n) // tps, 0, 0))]
    if emit_y:
        out_shape = [jax.ShapeDtypeStruct((B, T, D), x.dtype)] + out_shape
        out_specs = [pl.BlockSpec((S, TM, D), lambda n: (out_tile(n) // tps, out_tile(n) % tps, 0))] + out_specs

    M = S * TM
    lvl_rows = _scan_level_rows(TM)
    scratch_shapes = [
        pltpu.VMEM((S, d_rnn // LANES, cpad + TM, LANES), f32),
        pltpu.VMEM((S, d_conv // LANES, vpad + TM, LANES), f32),
        pltpu.VMEM((S, SUBLANES, d_rnn), f32),
        pltpu.VMEM((d_rnn // LANES, M, LANES), f32),
        pltpu.VMEM((d_rnn // LANES, M, LANES), f32),
        pltpu.VMEM((d_rnn // LANES, M, LANES), f32),
        pltpu.VMEM((2, S, d_rnn // LANES, max(sum(lvl_rows), SUBLANES), LANES), f32),
        pltpu.VMEM((S, d_rnn // LANES, max(sum(r + 2 * SUBLANES for r in lvl_rows), SUBLANES), LANES), f32),
        pltpu.VMEM((d_rnn // LANES, M, LANES), f32),
        pltpu.VMEM((d_conv // LANES, M, LANES), f32),
    ]
    w_names = _matmul_weights(emit_y)
    scratch_shapes += [pltpu.VMEM(params[k].shape, params[k].dtype) for k in w_names]
    scratch_shapes.append(pltpu.SemaphoreType.DMA((len(w_names),)))
    if pipelined:
        scratch_shapes.append(pltpu.VMEM((M, D), f32))
    kern = functools.partial(_layer_kernel, n_streams=S, tm=TM, n_tiles=n_tiles, tiles_per_stream=tps,
                             emit_y=emit_y, final_norm=final_norm, pipelined=pipelined,
                             rnn_conv_w=rnn_conv_w, dw_conv_w=dw_conv_w, ff_chunk=ff_chunk, conv_rows=conv_rows,
                             zero_state=zero_state)
    return pl.pallas_call(
        kern,
        grid=(n_tiles + (1 if pipelined else 0),),
        in_specs=in_specs,
        out_specs=out_specs,
        out_shape=out_shape,
        scratch_shapes=scratch_shapes,
        compiler_params=pltpu.CompilerParams(
            dimension_semantics=("arbitrary",),
            vmem_limit_bytes=VMEM_LIMIT_BYTES),
        name=f"hybrid_layer_s{S}_t{TM}",
    )(x, conv0, h0, dw0, *weights)


def _block_diag_gates(w_r, w_i):
    n_heads, hd, _ = w_r.shape
    per = MXU_DIM // hd
    n_slab = n_heads // per
    on_diag = jnp.eye(per, dtype=bool)[None, :, None, :, None]

    def slabs(w):
        w = w.reshape(n_slab, per, hd, hd)
        bd = jnp.where(on_diag, w[:, :, :, None, :], 0.0)
        return bd.reshape(n_slab, MXU_DIM, MXU_DIM)

    return jnp.concatenate([slabs(w_r), slabs(w_i)], axis=-1).astype(jnp.bfloat16)


def kernel(x_prompt, x_sample, state_rglru_conv, state_rglru_h, state_dwconv, meta_tokens, norm_mix, w_in,
           rnn_conv_w, rnn_conv_b, w_gate_r, b_gate_r, w_gate_i, b_gate_i, rglru_lambda, dw_w, dw_b, ln_conv_g,
           ln_conv_b, out_norm_rnn, out_norm_conv, w_out, norm_mlp, w_up, w_down, norm_final):
    depth = norm_mix.shape[0]
    bf16 = jnp.bfloat16

    xm = meta_tokens[None].astype(x_prompt.dtype)
    xp, xs = x_prompt, x_sample
    outs_p, outs_s = [], []
    for l in range(depth):
        last = l == depth - 1
        params = dict(
            norm_mix=norm_mix[l][None], w_in=w_in[l].astype(bf16), cw=rnn_conv_w[l], cb=rnn_conv_b[l][None],
            w_gate=_block_diag_gates(w_gate_r[l], w_gate_i[l]), b_r=b_gate_r[l][None], b_i=b_gate_i[l][None],
            lam=rglru_lambda[l][None], dww=dw_w[l], dwb=dw_b[l][None], ln_g=ln_conv_g[l][None],
            ln_b=ln_conv_b[l][None], onr=out_norm_rnn[l][None], onc=out_norm_conv[l][None],
            w_out=w_out[l].astype(bf16), norm_mlp=norm_mlp[l][None], w_up=w_up[l].astype(bf16),
            w_down=w_down[l].astype(bf16), norm_final=norm_final[None])
        res_m = _run_layer(xm, None, None, None, params, n_streams=1, tm=xm.shape[1], emit_y=not last,
                           final_norm=False)
        if not last:
            xm, res_m = res_m[0], res_m[1:]
        res_p = _run_layer(xp, *res_m, params, n_streams=1, tm=min(512, xp.shape[1]), emit_y=True,
                           final_norm=last)
        res_s = _run_layer(xs, state_rglru_conv[l], state_rglru_h[l][:, None], state_dwconv[l], params,
                           n_streams=xs.shape[0], tm=xs.shape[1], emit_y=True, final_norm=last)
        xp, xs = res_p[0], res_s[0]
        outs_p.append(res_p[1:])
        outs_s.append(res_s[1:])

    def stack(outs, i):
        return outs[0][i][None] if depth == 1 else jnp.stack([o[i] for o in outs])

    return (xp, xs,
            stack(outs_p, 0), stack(outs_p, 1)[:, :, 0], stack(outs_p, 2),
            stack(outs_s, 0), stack(outs_s, 1)[:, :, 0], stack(outs_s, 2))
```

```python
import functools
import math

import jax
import jax.numpy as jnp
from jax import lax
from jax.experimental import pallas as pl
from jax.experimental.pallas import tpu as pltpu

EPS = 1e-6
RGLRU_C = 8.0
SUBLANES = 8
LANES = 128
MXU_DIM = 256
VMEM_LIMIT_BYTES = 60000 * 1024


def _rms(x, g):
    ms = jnp.mean(x * x, axis=-1, keepdims=True)
    return x * lax.rsqrt(ms + EPS) * g


def _sigmoid(x):
    return 0.5 * jnp.tanh(0.5 * x) + 0.5


def _silu(x):
    h = 0.5 * x
    return h * (jnp.tanh(h) + 1.0)


def _gelu_tanh(x):
    c = math.sqrt(2.0 / math.pi)
    inner = x * (c + (0.044715 * c) * (x * x))
    return (0.5 * x) * (1.0 + jnp.tanh(inner))


def _bf16_dot(a, b):
    return jnp.dot(a.astype(jnp.bfloat16), b, preferred_element_type=jnp.float32)


def _to_slabs(dst, idx, x):
    for c in range(x.shape[-1] // LANES):
        dst[idx + (c,)] = x[:, c * LANES:(c + 1) * LANES]


def _from_slabs(buf):
    return jnp.concatenate([buf[c] for c in range(buf.shape[0])], axis=-1)


def _strided_causal_conv(win, s, w_ref, b_ref, out, *, taps, lead, tm, rows):
    half = rows // 2
    for c in range(win.shape[1]):
        lanes = slice(c * LANES, (c + 1) * LANES)
        for t0 in [i * rows + par for i in range(tm // rows) for par in range(2)]:
            acc = win[s, c, pl.ds(lead + t0, half, stride=2), :] * w_ref[0:1, lanes]
            for k in range(1, taps):
                acc = acc + win[s, c, pl.ds(lead + t0 + k, half, stride=2), :] * w_ref[k:k + 1, lanes]
            out[c, pl.ds(s * tm + t0, half, stride=2), :] = acc + b_ref[:, lanes]


def _scan_radices(n_rows):
    n_groups, radices = n_rows // SUBLANES, []
    while n_groups > 1:
        assert n_groups % 2 == 0
        r = 4 if n_groups % 4 == 0 else 2
        radices.append(r)
        n_groups //= r
    return radices


def _scan_level_rows(n_rows):
    rows, out = n_rows, []
    for r in _scan_radices(n_rows)[:-1]:
        rows //= r
        out.append(rows)
    return out


def _linear_scan_slab(a_ref, b_ref, h_ref, ab_lvl, hin_lvl, base, n_rows, carry):
    radices = _scan_radices(n_rows)
    lvl_rows = _scan_level_rows(n_rows)
    ab_off = [sum(lvl_rows[:i]) for i in range(len(lvl_rows))]
    hin_off = [sum(r + 2 * SUBLANES for r in lvl_rows[:i]) for i in range(len(lvl_rows))]

    partial = []
    rows = n_rows
    for lvl, r in enumerate(radices):
        g = rows // r
        src, off = ((a_ref, b_ref), base) if lvl == 0 else ((ab_lvl.at[0], ab_lvl.at[1]), ab_off[lvl - 1])
        ld = lambda kind, k: src[kind][pl.ds(off + k, g, stride=r), :]
        pa, pb = ld(0, 0), ld(1, 0)
        maps = [(pa, pb)]
        for k in range(1, r):
            ak, bk = ld(0, k), ld(1, k)
            pa, pb = ak * pa, ak * pb + bk
            maps.append((pa, pb))
        partial.append(maps)
        rows = g
        if lvl + 1 < len(radices):
            ab_lvl[0, ab_off[lvl]:ab_off[lvl] + g, :] = pa
            ab_lvl[1, ab_off[lvl]:ab_off[lvl] + g, :] = pb
    if not radices:
        pa, pb = a_ref[base:base + SUBLANES, :], b_ref[base:base + SUBLANES, :]

    row = lax.broadcasted_iota(jnp.int32, (SUBLANES, LANES), 0)
    for sh in (1, 2, 4):
        keep = row >= sh
        pa_s = jnp.where(keep, pltpu.roll(pa, sh, 0), 1.0)
        pb_s = jnp.where(keep, pltpu.roll(pb, sh, 0), 0.0)
        pa, pb = pa * pa_s, pa * pb_s + pb
    h_top = pa * carry + pb
    new_carry = jnp.broadcast_to(h_top[SUBLANES - 1:SUBLANES, :], (SUBLANES, LANES))
    if not radices:
        h_ref[base:base + SUBLANES, :] = h_top
        return new_carry
    h_prev = jnp.where(row == 0, carry, pltpu.roll(h_top, 1, 0))

    for lvl in reversed(range(len(radices))):
        r = radices[lvl]
        g = partial[lvl][0][0].shape[0]
        if lvl < len(radices) - 1:
            o = hin_off[lvl] + SUBLANES
            h_prev = hin_lvl[o:o + g, :]
        for k, (pa, pb) in enumerate(partial[lvl]):
            hk = pa * h_prev + pb
            if lvl == 0:
                h_ref[pl.ds(base + k, g, stride=r), :] = hk
            else:
                o = hin_off[lvl - 1] + SUBLANES
                hin_lvl[pl.ds(o + 1 + k, g, stride=r), :] = hk
        if lvl > 0:
            o = hin_off[lvl - 1] + SUBLANES
            hin_lvl[o:o + 1, :] = carry[0:1, :]
    return new_carry


def _layer_kernel(x_ref, conv0_ref, h0_ref, dw0_ref,
                  norm_mix_ref, w_in_ref, cw_ref, cb_ref, w_gate_ref, b_r_ref, b_i_ref, lam_ref,
                  dww_ref, dwb_ref, ln_g_ref, ln_b_ref, onr_ref, onc_ref,
                  w_out_ref, norm_mlp_ref, w_up_ref, w_down_ref, norm_final_ref,
                  *rest, n_streams, tm, n_tiles, tiles_per_stream, emit_y, final_norm, pipelined,
                  rnn_conv_w, dw_conv_w, ff_chunk, conv_rows, zero_state):
    if emit_y:
        y_ref, conv_out_ref, h_out_ref, dw_out_ref = rest[:4]
        scratch = rest[4:]
    else:
        conv_out_ref, h_out_ref, dw_out_ref = rest[:3]
        scratch = rest[3:]
    cbuf, vbuf, hcar, a_buf, b_buf, h_buf, ab_lvl, hin_lvl, xc_buf, vc_buf = scratch[:10]

    w_hbm = dict(w_in=w_in_ref, w_gate=w_gate_ref, w_out=w_out_ref, w_up=w_up_ref, w_down=w_down_ref)
    w_names = _matmul_weights(emit_y)
    w_vmem = dict(zip(w_names, scratch[10:10 + len(w_names)]))
    w_sem = scratch[10 + len(w_names)]
    scratch = scratch[:10] + scratch[11 + len(w_names):]
    in_flight = set()

    def weight_copy(name):
        return pltpu.make_async_copy(w_hbm[name], w_vmem[name], w_sem.at[w_names.index(name)])

    def start_weight_copies():
        for name in w_names:
            weight_copy(name).start()
            in_flight.add(name)

    def finish_weight_copies():
        for name in w_names:
            weight(name)

    def weight(name):
        if name in in_flight:
            weight_copy(name).wait()
            in_flight.discard(name)
        return w_vmem[name]

    S, TM = n_streams, tm
    M = S * TM
    d_rnn = cw_ref.shape[-1]
    d_conv = dww_ref.shape[-1]
    cpad = cbuf.shape[2] - TM
    vpad = vbuf.shape[2] - TM
    c_hist = rnn_conv_w - 1
    v_hist = dw_conv_w - 1
    n = pl.program_id(0)
    tile = jnp.minimum(n, n_tiles - 1)
    live = n < n_tiles

    @pl.when(jnp.logical_and(lax.rem(tile, tiles_per_stream) == 0, live))
    def _load_state():
        if zero_state:
            cbuf[:, :, 0:cpad, :] = jnp.zeros((S, d_rnn // LANES, cpad, LANES), jnp.float32)
            vbuf[:, :, 0:vpad, :] = jnp.zeros((S, d_conv // LANES, vpad, LANES), jnp.float32)
            hcar[...] = jnp.zeros(hcar.shape, jnp.float32)
            return
        cbuf[:, :, 0:cpad - c_hist, :] = jnp.zeros((S, d_rnn // LANES, cpad - c_hist, LANES), jnp.float32)
        vbuf[:, :, 0:vpad - v_hist, :] = jnp.zeros((S, d_conv // LANES, vpad - v_hist, LANES), jnp.float32)
        for c in range(d_rnn // LANES):
            cbuf[:, c, cpad - c_hist:cpad, :] = conv0_ref[:, :, c * LANES:(c + 1) * LANES]
        for c in range(d_conv // LANES):
            vbuf[:, c, vpad - v_hist:vpad, :] = dw0_ref[:, :, c * LANES:(c + 1) * LANES]
        hcar[...] = jnp.broadcast_to(h0_ref[...], hcar.shape)

    def in_proj():
        x = x_ref[...].reshape(M, x_ref.shape[-1])
        hn = _rms(x, norm_mix_ref[...])
        return _bf16_dot(hn, weight("w_in")[...])

    n_gate_slabs = d_rnn // MXU_DIM
    lanes_per_gate_slab = MXU_DIM // LANES

    def short_conv(z):
        xr = z[:, 0:d_rnn]
        for s in range(S):
            _to_slabs(cbuf.at[:, :, cpad:cpad + TM, :], (s,), xr[s * TM:(s + 1) * TM, :])
            _strided_causal_conv(cbuf, s, cw_ref, cb_ref, xc_buf, taps=rnn_conv_w, lead=cpad - c_hist, tm=TM,
                                 rows=conv_rows)
            for c in range(d_rnn // LANES):
                cbuf[s, c, 0:cpad, :] = cbuf[s, c, TM:TM + cpad, :]

    def gate_proj(j):
        slabs = range(j * lanes_per_gate_slab, (j + 1) * lanes_per_gate_slab)
        xc = jnp.concatenate([xc_buf[c] for c in slabs], axis=-1)
        return xc, _bf16_dot(xc, weight("w_gate")[j])

    def rglru(j, xc, ri):
        lanes = slice(j * MXU_DIM, (j + 1) * MXU_DIM)
        r = _sigmoid(ri[:, :MXU_DIM] + b_r_ref[:, lanes])
        ig = _sigmoid(ri[:, MXU_DIM:] + b_i_ref[:, lanes])
        nl = -lam_ref[:, lanes]
        softplus_nl = jnp.maximum(nl, 0.0) + jnp.log1p(jnp.exp(-jnp.abs(nl)))
        a = jnp.exp((-RGLRU_C * r) * softplus_nl)
        b = jnp.sqrt(jnp.maximum(1.0 - a * a, 0.0)) * ig * xc
        for k in range(lanes_per_gate_slab):
            c = j * lanes_per_gate_slab + k
            a_buf[c] = a[:, k * LANES:(k + 1) * LANES]
            b_buf[c] = b[:, k * LANES:(k + 1) * LANES]
            for s in range(S):
                hcar[s, :, c * LANES:(c + 1) * LANES] = _linear_scan_slab(
                    a_buf.at[c], b_buf.at[c], h_buf.at[c], ab_lvl.at[:, s, c], hin_lvl.at[s, c], s * TM, TM,
                    hcar[s, :, c * LANES:(c + 1) * LANES])

    def mixer_rest(z):
        x = x_ref[...].reshape(M, x_ref.shape[-1])
        gate = z[:, d_rnn:2 * d_rnn]
        glu_v = z[:, 2 * d_rnn:2 * d_rnn + d_conv]
        glu_g = z[:, 2 * d_rnn + d_conv:]

        v = glu_v * _sigmoid(glu_g)
        for s in range(S):
            _to_slabs(vbuf.at[:, :, vpad:vpad + TM, :], (s,), v[s * TM:(s + 1) * TM, :])
            _strided_causal_conv(vbuf, s, dww_ref, dwb_ref, vc_buf, taps=dw_conv_w, lead=vpad - v_hist, tm=TM,
                                 rows=conv_rows)
            for c in range(d_conv // LANES):
                vbuf[s, c, 0:vpad, :] = vbuf[s, c, TM:TM + vpad, :]
        if not emit_y:
            return None

        y_rnn = _from_slabs(h_buf) * _gelu_tanh(gate)
        vc = _from_slabs(vc_buf)
        mu = jnp.mean(vc, axis=-1, keepdims=True)
        vcc = vc - mu
        ln = vcc * lax.rsqrt(jnp.mean(vcc * vcc, axis=-1, keepdims=True) + EPS) * ln_g_ref[...] + ln_b_ref[...]
        y_conv = _silu(ln)
        mix = jnp.concatenate([_rms(y_rnn, onr_ref[...]), _rms(y_conv, onc_ref[...])], axis=-1)
        return x + _bf16_dot(mix, weight("w_out")[...])

    def mixer():
        z = in_proj()
        short_conv(z)
        for j in range(n_gate_slabs):
            rglru(j, *gate_proj(j))
        return mixer_rest(z)

    n_ff = w_up_ref.shape[-1] // ff_chunk

    def mlp_chunk(hm, x2, c):
        hc = jnp.dot(hm, weight("w_up")[:, c * ff_chunk:(c + 1) * ff_chunk], preferred_element_type=jnp.float32)
        hc = jnp.square(jnp.maximum(hc, 0.0))
        return x2 + _bf16_dot(hc, weight("w_down")[c * ff_chunk:(c + 1) * ff_chunk, :])

    def mlp_begin(x1, n_chunks):
        hm = _rms(x1, norm_mlp_ref[...]).astype(jnp.bfloat16)
        x2 = x1
        for c in range(n_chunks):
            x2 = mlp_chunk(hm, x2, c)
        return hm, x2, n_chunks

    def mlp_finish(hm, x2, n_done):
        for c in range(n_done, n_ff):
            x2 = mlp_chunk(hm, x2, c)
        if final_norm:
            x2 = _rms(x2, norm_final_ref[...])
        y_ref[...] = x2.reshape(y_ref.shape)

    def store_state():
        for s in range(S):
            for c in range(d_rnn // LANES):
                conv_out_ref[s, :, c * LANES:(c + 1) * LANES] = cbuf[s, c, cpad - c_hist:cpad, :]
            for c in range(d_conv // LANES):
                dw_out_ref[s, :, c * LANES:(c + 1) * LANES] = vbuf[s, c, vpad - v_hist:vpad, :]
            h_out_ref[s] = hcar[s, 0:1, :]

    if pipelined:
        x1_buf = scratch[10]

        @pl.when(n == 0)
        def _first_step():
            start_weight_copies()
            x1_buf[...] = mixer()
            finish_weight_copies()

        @pl.when(jnp.logical_and(n > 0, live))
        def _steady_step():
            z = in_proj()
            first_gate = max(n_ff - n_gate_slabs, 0)
            hm, x2, _ = mlp_begin(x1_buf[...], first_gate)
            short_conv(z)
            for j in range(n_gate_slabs):
                gate_out = gate_proj(j)
                if first_gate + j < n_ff:
                    x2 = mlp_chunk(hm, x2, first_gate + j)
                rglru(j, *gate_out)
            mlp_finish(hm, x2, min(first_gate + n_gate_slabs, n_ff))
            x1_buf[...] = mixer_rest(z)

        @pl.when(n == n_tiles)
        def _last_step():
            mlp_finish(*mlp_begin(x1_buf[...], 0))

        pl.when(live)(store_state)
    else:
        if n_tiles == 1:
            start_weight_copies()
        else:
            @pl.when(n == 0)
            def _load_weights():
                start_weight_copies()
                finish_weight_copies()
        x1 = mixer()
        if emit_y:
            mlp_finish(*mlp_begin(x1, 0))
        finish_weight_copies()
        store_state()


def _matmul_weights(emit_y):
    return ("w_in", "w_gate") + (("w_out", "w_up", "w_down") if emit_y else ())


def _const_spec(arr):
    nd = arr.ndim
    return pl.BlockSpec(arr.shape, lambda n, _nd=nd: (0,) * _nd, pipeline_mode=pl.Buffered(1))


def _run_layer(x, conv0, h0, dw0, params, *, n_streams, tm, emit_y, final_norm):
    B, T, D = x.shape
    S, TM = n_streams, tm
    assert B % S == 0 and T % TM == 0 and TM % SUBLANES == 0
    rnn_conv_w, d_rnn = params["cw"].shape
    dw_conv_w, d_conv = params["dww"].shape
    assert d_rnn % MXU_DIM == 0 and d_conv % LANES == 0
    cpad = -(-(rnn_conv_w - 1) // SUBLANES) * SUBLANES
    vpad = -(-(dw_conv_w - 1) // SUBLANES) * SUBLANES
    conv_rows = min(256, TM)
    assert TM % conv_rows == 0 and conv_rows % (2 * SUBLANES) == 0
    d_ff = params["w_up"].shape[-1]
    ff_chunk = min(1024, d_ff)
    tps = T // TM
    n_tiles = (B // S) * tps
    pipelined = emit_y and n_tiles > 1

    def in_tile(n):
        return jnp.minimum(n, n_tiles - 1) if pipelined else n

    def out_tile(n):
        return jnp.maximum(n - 1, 0) if pipelined else n

    zero_state = conv0 is None
    if zero_state:
        assert h0 is None and dw0 is None
        conv0 = h0 = dw0 = params["cb"]

    def state_spec(arr):
        if zero_state:
            return pl.BlockSpec(memory_space=pl.ANY)
        blk = (S,) + arr.shape[1:]
        if arr.shape[0] == B:
            return pl.BlockSpec(blk, lambda n: (in_tile(n) // tps, 0, 0))
        assert arr.shape[0] == 1 and S == 1
        return pl.BlockSpec(blk, lambda n: (0, 0, 0))

    order = ["norm_mix", "w_in", "cw", "cb", "w_gate", "b_r", "b_i", "lam", "dww", "dwb", "ln_g", "ln_b",
             "onr", "onc", "w_out", "norm_mlp", "w_up", "w_down", "norm_final"]
    weights = [params[k] for k in order]
    in_specs = [pl.BlockSpec((S, TM, D), lambda n: (in_tile(n) // tps, in_tile(n) % tps, 0)),
                state_spec(conv0), state_spec(h0), state_spec(dw0)]
    in_hbm = ("w_in", "w_gate", "w_out", "w_up", "w_down")
    in_specs += [pl.BlockSpec(memory_space=pl.ANY) if k in in_hbm else _const_spec(w) for k, w in zip(order, weights)]

    f32 = jnp.float32
    out_shape = [jax.ShapeDtypeStruct((B, rnn_conv_w - 1, d_rnn), f32),
                 jax.ShapeDtypeStruct((B, 1, d_rnn), f32),
                 jax.ShapeDtypeStruct((B, dw_conv_w - 1, d_conv), f32)]
    out_specs = [pl.BlockSpec((S, rnn_conv_w - 1, d_rnn), lambda n: (in_tile(n) // tps, 0, 0)),
                 pl.BlockSpec((S, 1, d_rnn), lambda n: (in_tile(n) // tps, 0, 0)),
                 pl.BlockSpec((S, dw_conv_w - 1, d_conv), lambda n: (in_tile(n) // tps, 0, 0))]
    if emit_y:
        out_shape = [jax.ShapeDtypeStruct((B, T, D), x.dtype)] + out_shape
        out_specs = [pl.BlockSpec((S, TM, D), lambda n: (out_tile(n) // tps, out_tile(n) % tps, 0))] + out_specs

    M = S * TM
    lvl_rows = _scan_level_rows(TM)
    scratch_shapes = [
        pltpu.VMEM((S, d_rnn // LANES, cpad + TM, LANES), f32),
        pltpu.VMEM((S, d_conv // LANES, vpad + TM, LANES), f32),
        pltpu.VMEM((S, SUBLANES, d_rnn), f32),
        pltpu.VMEM((d_rnn // LANES, M, LANES), f32),
        pltpu.VMEM((d_rnn // LANES, M, LANES), f32),
        pltpu.VMEM((d_rnn // LANES, M, LANES), f32),
        pltpu.VMEM((2, S, d_rnn // LANES, max(sum(lvl_rows), SUBLANES), LANES), f32),
        pltpu.VMEM((S, d_rnn // LANES, max(sum(r + 2 * SUBLANES for r in lvl_rows), SUBLANES), LANES), f32),
        pltpu.VMEM((d_rnn // LANES, M, LANES), f32),
        pltpu.VMEM((d_conv // LANES, M, LANES), f32),
    ]
    w_names = _matmul_weights(emit_y)
    scratch_shapes += [pltpu.VMEM(params[k].shape, params[k].dtype) for k in w_names]
    scratch_shapes.append(pltpu.SemaphoreType.DMA((len(w_names),)))
    if pipelined:
        scratch_shapes.append(pltpu.VMEM((M, D), f32))
    kern = functools.partial(_layer_kernel, n_streams=S, tm=TM, n_tiles=n_tiles, tiles_per_stream=tps,
                             emit_y=emit_y, final_norm=final_norm, pipelined=pipelined,
                             rnn_conv_w=rnn_conv_w, dw_conv_w=dw_conv_w, ff_chunk=ff_chunk, conv_rows=conv_rows,
                             zero_state=zero_state)
    return pl.pallas_call(
        kern,
        grid=(n_tiles + (1 if pipelined else 0),),
        in_specs=in_specs,
        out_specs=out_specs,
        out_shape=out_shape,
        scratch_shapes=scratch_shapes,
        compiler_params=pltpu.CompilerParams(
            dimension_semantics=("arbitrary",),
            vmem_limit_bytes=VMEM_LIMIT_BYTES),
        name=f"hybrid_layer_s{S}_t{TM}",
    )(x, conv0, h0, dw0, *weights)


def _block_diag_gates(w_r, w_i):
    n_heads, hd, _ = w_r.shape
    per = MXU_DIM // hd
    n_slab = n_heads // per
    on_diag = jnp.eye(per, dtype=bool)[None, :, None, :, None]

    def slabs(w):
        w = w.reshape(n_slab, per, hd, hd)
        bd = jnp.where(on_diag, w[:, :, :, None, :], 0.0)
        return bd.reshape(n_slab, MXU_DIM, MXU_DIM)

    return jnp.concatenate([slabs(w_r), slabs(w_i)], axis=-1).astype(jnp.bfloat16)


def kernel(x_prompt, x_sample, state_rglru_conv, state_rglru_h, state_dwconv, meta_tokens, norm_mix, w_in,
           rnn_conv_w, rnn_conv_b, w_gate_r, b_gate_r, w_gate_i, b_gate_i, rglru_lambda, dw_w, dw_b, ln_conv_g,
           ln_conv_b, out_norm_rnn, out_norm_conv, w_out, norm_mlp, w_up, w_down, norm_final):
    depth = norm_mix.shape[0]
    bf16 = jnp.bfloat16

    xm = meta_tokens[None].astype(x_prompt.dtype)
    xp, xs = x_prompt, x_sample
    outs_p, outs_s = [], []
    for l in range(depth):
        last = l == depth - 1
        params = dict(
            norm_mix=norm_mix[l][None], w_in=w_in[l].astype(bf16), cw=rnn_conv_w[l], cb=rnn_conv_b[l][None],
            w_gate=_block_diag_gates(w_gate_r[l], w_gate_i[l]), b_r=b_gate_r[l][None], b_i=b_gate_i[l][None],
            lam=rglru_lambda[l][None], dww=dw_w[l], dwb=dw_b[l][None], ln_g=ln_conv_g[l][None],
            ln_b=ln_conv_b[l][None], onr=out_norm_rnn[l][None], onc=out_norm_conv[l][None],
            w_out=w_out[l].astype(bf16), norm_mlp=norm_mlp[l][None], w_up=w_up[l].astype(bf16),
            w_down=w_down[l].astype(bf16), norm_final=norm_final[None])
        res_m = _run_layer(xm, None, None, None, params, n_streams=1, tm=xm.shape[1], emit_y=not last,
                           final_norm=False)
        if not last:
            xm, res_m = res_m[0], res_m[1:]
        res_p = _run_layer(xp, *res_m, params, n_streams=1, tm=min(512, xp.shape[1]), emit_y=True,
                           final_norm=last)
        s_streams = xs.shape[0] // 2 if xs.shape[0] % 2 == 0 else xs.shape[0]
        res_s = _run_layer(xs, state_rglru_conv[l], state_rglru_h[l][:, None], state_dwconv[l], params,
                           n_streams=s_streams, tm=xs.shape[1], emit_y=True, final_norm=last)
        xp, xs = res_p[0], res_s[0]
        outs_p.append(res_p[1:])
        outs_s.append(res_s[1:])

    def stack(outs, i):
        return outs[0][i][None] if depth == 1 else jnp.stack([o[i] for o in outs])

    return (xp, xs,
            stack(outs_p, 0), stack(outs_p, 1)[:, :, 0], stack(outs_p, 2),
            stack(outs_s, 0), stack(outs_s, 1)[:, :, 0], stack(outs_s, 2))
```
